```python
import math
import jax, jax.numpy as jnp
from jax import lax
import numpy as np

D_MODEL = 1024
BATCH = 16
SEQ = 4096
DEPTH = 1
DEC_BATCH = 32
DEC_SEQ = 2048
PAST_LEN = 128

N_META = 16
N_HEADS = 8
NOPE_DIM = 128
ROPE_DIM = 64
V_DIM = 128
Q_LORA = 512
KV_LORA = 256
ROPE_THETA = 10000.0
Q_BLOCK = 128
D_CONV = D_MODEL
CONV_WIDTH = 31
CONV_PAD = CONV_WIDTH // 2
D_FF = 4 * D_MODEL
N_GATE = 2 * D_MODEL
EPS = 1e-6
SPLITS = [Q_LORA, Q_LORA + KV_LORA, Q_LORA + KV_LORA + ROPE_DIM,
          Q_LORA + KV_LORA + ROPE_DIM + 2 * D_CONV]
N_IN = Q_LORA + KV_LORA + ROPE_DIM + 2 * D_CONV + N_GATE

kernel_name = "mla_conformer_gated_hybrid_encoder"


def _rms(x, g):
    xf = x.astype(jnp.float32)
    y = xf * lax.rsqrt(jnp.mean(xf * xf, axis=-1, keepdims=True) + EPS)
    return (y * g.astype(jnp.float32)).astype(x.dtype)


def _layernorm(x, g, b):
    xf = x.astype(jnp.float32)
    mu = jnp.mean(xf, axis=-1, keepdims=True)
    var = jnp.mean(jnp.square(xf - mu), axis=-1, keepdims=True)
    y = (xf - mu) * lax.rsqrt(var + EPS)
    return (y * g.astype(jnp.float32) + b.astype(jnp.float32)).astype(x.dtype)


def _rope_cs(L):
    inv = ROPE_THETA ** (-jnp.arange(0, ROPE_DIM, 2, dtype=jnp.float32) / ROPE_DIM)
    ang = jnp.arange(L, dtype=jnp.float32)[:, None] * inv[None, :]
    return jnp.cos(ang), jnp.sin(ang)


def _apply_rope(x, cos, sin):
    x1, x2 = jnp.split(x, 2, axis=-1)
    c = cos.astype(x.dtype)
    s = sin.astype(x.dtype)
    return jnp.concatenate([x1 * c - x2 * s, x2 * c + x1 * s], axis=-1)


def _mla_attention(q_n, q_r, k_n, k_r, v):
    B, L, H, _ = q_n.shape
    nblk = -(-L // Q_BLOCK)
    pad = nblk * Q_BLOCK - L
    scale = 1.0 / math.sqrt(NOPE_DIM + ROPE_DIM)

    def blocks(t):
        t = jnp.pad(t, ((0, 0), (0, pad), (0, 0), (0, 0)))
        return t.reshape(B, nblk, Q_BLOCK, H, t.shape[-1]).transpose(1, 0, 2, 3, 4)

    def one(args):
        qn_b, qr_b = args
        s = (jnp.einsum('bqhd,bkhd->bhqk', qn_b, k_n)
             + jnp.einsum('bqhr,bkr->bhqk', qr_b, k_r))
        p = jax.nn.softmax(s.astype(jnp.float32) * scale, axis=-1).astype(v.dtype)
        return jnp.einsum('bhqk,bkhd->bqhd', p, v)

    o = lax.map(one, (blocks(q_n), blocks(q_r)))
    o = o.transpose(1, 0, 2, 3, 4).reshape(B, nblk * Q_BLOCK, H, V_DIM)[:, :L]
    return o


def _encode(x, meta_tokens, norm_mix, w_in, b_gate, norm_q, w_uq, norm_kv, w_ukv,
            w_o_attn, conv_w, conv_b, conv_ln_g, conv_ln_b, w_conv_out, w_out,
            norm_mlp, w_mlp_in, w_mlp_out, norm_final):
    B = x.shape[0]
    meta = jnp.broadcast_to(meta_tokens.astype(x.dtype)[None], (B, N_META, D_MODEL))
    h = jnp.concatenate([meta, x], axis=1)
    L = h.shape[1]
    cos, sin = _rope_cs(L)
    for l in range(DEPTH):
        u = _rms(h, norm_mix[l])
        proj = u @ w_in[l]
        c_q, c_kv, k_r, conv_in, gate_logits = jnp.split(proj, SPLITS, axis=-1)

        q = (_rms(c_q, norm_q[l]) @ w_uq[l]).reshape(B, L, N_HEADS, NOPE_DIM + ROPE_DIM)
        q_n = q[..., :NOPE_DIM]
        q_r = _apply_rope(q[..., NOPE_DIM:], cos[:, None, :], sin[:, None, :])
        kv = (_rms(c_kv, norm_kv[l]) @ w_ukv[l]).reshape(B, L, N_HEADS, NOPE_DIM + V_DIM)
        k_n = kv[..., :NOPE_DIM]
        v = kv[..., NOPE_DIM:]
        k_r = _apply_rope(k_r, cos, sin)
        attn = _mla_attention(q_n, q_r, k_n, k_r, v).reshape(B, L, N_HEADS * V_DIM) @ w_o_attn[l]

        a, b = jnp.split(conv_in, 2, axis=-1)
        c = a * jax.nn.sigmoid(b)
        c = lax.conv_general_dilated(
            c, conv_w[l][:, None, :].astype(c.dtype), window_strides=(1,),
            padding=[(CONV_PAD, CONV_PAD)], dimension_numbers=('NWC', 'WIO', 'NWC'),
            feature_group_count=D_CONV) + conv_b[l]
        c = jax.nn.silu(_layernorm(c, conv_ln_g[l], conv_ln_b[l]))
        conv = c @ w_conv_out[l]

        g_a, g_c = jnp.split(jax.nn.sigmoid(gate_logits + b_gate[l]), 2, axis=-1)
        h = h + (g_a * attn + g_c * conv) @ w_out[l]

        u = _rms(h, norm_mlp[l])
        h = h + jnp.square(jax.nn.relu(u @ w_mlp_in[l])) @ w_mlp_out[l]
    return _rms(h, norm_final)[:, N_META:]


def setup_inputs(seed: int = 0) -> dict:
    key = jax.random.key(seed)
    ks = jax.random.split(key, 24)
    f32 = jnp.float32

    def nrm(k, shape, scale):
        return jax.random.normal(k, shape, f32) * scale

    def gain(k, shape):
        return 1.0 + 0.02 * jax.random.normal(k, shape, f32)

    return {
        "x_prompt": jax.random.normal(ks[0], (BATCH, SEQ, D_MODEL), f32),
        "x_sample": jax.random.normal(ks[1], (DEC_BATCH, DEC_SEQ, D_MODEL), f32),
        "meta_tokens": nrm(ks[2], (N_META, D_MODEL), 1.0),
        "norm_mix": gain(ks[3], (DEPTH, D_MODEL)),
        "w_in": nrm(ks[4], (DEPTH, D_MODEL, N_IN), D_MODEL ** -0.5),
        "b_gate": nrm(ks[5], (DEPTH, N_GATE), 0.01),
        "norm_q": gain(ks[6], (DEPTH, Q_LORA)),
        "w_uq": nrm(ks[7], (DEPTH, Q_LORA, N_HEADS * (NOPE_DIM + ROPE_DIM)), Q_LORA ** -0.5),
        "norm_kv": gain(ks[8], (DEPTH, KV_LORA)),
        "w_ukv": nrm(ks[9], (DEPTH, KV_LORA, N_HEADS * (NOPE_DIM + V_DIM)), KV_LORA ** -0.5),
        "w_o_attn": nrm(ks[10], (DEPTH, N_HEADS * V_DIM, D_MODEL), (N_HEADS * V_DIM) ** -0.5),
        "conv_w": nrm(ks[11], (DEPTH, CONV_WIDTH, D_CONV), CONV_WIDTH ** -0.5),
        "conv_b": nrm(ks[12], (DEPTH, D_CONV), 0.01),
        "conv_ln_g": gain(ks[13], (DEPTH, D_CONV)),
        "conv_ln_b": nrm(ks[14], (DEPTH, D_CONV), 0.01),
        "w_conv_out": nrm(ks[15], (DEPTH, D_CONV, D_MODEL), D_CONV ** -0.5),
        "w_out": nrm(ks[16], (DEPTH, D_MODEL, D_MODEL), D_MODEL ** -0.5),
        "norm_mlp": gain(ks[17], (DEPTH, D_MODEL)),
        "w_mlp_in": nrm(ks[18], (DEPTH, D_MODEL, D_FF), D_MODEL ** -0.5),
        "w_mlp_out": nrm(ks[19], (DEPTH, D_FF, D_MODEL), D_FF ** -0.5),
        "norm_final": gain(ks[20], (D_MODEL,)),
    }


def reference(x_prompt, x_sample, meta_tokens, norm_mix, w_in, b_gate, norm_q, w_uq,
              norm_kv, w_ukv, w_o_attn, conv_w, conv_b, conv_ln_g, conv_ln_b, w_conv_out,
              w_out, norm_mlp, w_mlp_in, w_mlp_out, norm_final):
    y_prompt = _encode(x_prompt, meta_tokens, norm_mix, w_in, b_gate, norm_q, w_uq, norm_kv,
                       w_ukv, w_o_attn, conv_w, conv_b, conv_ln_g, conv_ln_b, w_conv_out,
                       w_out, norm_mlp, w_mlp_in, w_mlp_out, norm_final)
    y_sample = _encode(x_sample, meta_tokens, norm_mix, w_in, b_gate, norm_q, w_uq, norm_kv,
                       w_ukv, w_o_attn, conv_w, conv_b, conv_ln_g, conv_ln_b, w_conv_out,
                       w_out, norm_mlp, w_mlp_in, w_mlp_out, norm_final)
    return (y_prompt, y_sample)
```

```python
import functools
import math

import jax
import jax.numpy as jnp
from jax import lax
from jax.experimental import pallas as pl
from jax.experimental.pallas import tpu as pltpu

D_MODEL = 1024
N_META = 16
N_HEADS = 8
NOPE_DIM = 128
ROPE_DIM = 64
V_DIM = 128
Q_LORA = 512
KV_LORA = 256
ROPE_THETA = 10000.0
D_CONV = D_MODEL
CONV_WIDTH = 31
CONV_PAD = CONV_WIDTH // 2
D_FF = 4 * D_MODEL
N_GATE = 2 * D_MODEL
EPS = 1e-6

LANES = 128
HEAD_W = 2 * LANES
META_ROWS = LANES
HALO = 16
LAT_W = Q_LORA + KV_LORA + LANES
Q_SCALE = math.log2(math.e) / math.sqrt(NOPE_DIM + ROPE_DIM)
NEG_BIG = -1e30
VMEM_LIMIT = 56 * 1024 * 1024

F32 = jnp.float32
BF16 = jnp.bfloat16


def _rms_f32(x, g):
    return x * lax.rsqrt(jnp.mean(x * x, axis=-1, keepdims=True) + EPS) * g


def _dot(a, b):
    return jnp.dot(a, b, preferred_element_type=F32)


def _dot_nt(a, b):
    return lax.dot_general(a, b, (((1,), (1,)), ((), ())), preferred_element_type=F32)


def _rope_pair_sum(t):
    return t + pltpu.roll(t, ROPE_DIM, 1)


def _proj_kernel(x_ref, tab_ref, gmix_ref, wa_ref, gq_ref, wq_ref, gkv_ref, wkv_ref,
                 q_ref, k_ref, v_ref, c_ref):
    x = x_ref[0]
    u = _rms_f32(x, gmix_ref[...]).astype(BF16)
    tab = tab_ref[...]
    lat = _dot(u, wa_ref[:, :LAT_W])

    rq = _rms_f32(lat[:, :Q_LORA], gq_ref[...]).astype(BF16)
    qa = _dot(rq, wq_ref[...])
    for h in range(N_HEADS):
        base = h * HEAD_W
        qn = qa[:, base:base + LANES] * Q_SCALE
        qr = _rope_pair_sum(qa[:, base + LANES:base + HEAD_W] * tab) * Q_SCALE
        q_ref[0, h, :, :LANES] = qn.astype(BF16)
        q_ref[0, h, :, LANES:] = qr.astype(BF16)

    kr = _rope_pair_sum(lat[:, Q_LORA + KV_LORA:LAT_W] * tab)
    lane = lax.broadcasted_iota(jnp.int32, kr.shape, 1)
    kr = jnp.where(lane < ROPE_DIM, kr, 0.0).astype(BF16)

    rkv = _rms_f32(lat[:, Q_LORA:Q_LORA + KV_LORA], gkv_ref[...]).astype(BF16)
    kv = _dot(rkv, wkv_ref[...])
    for h in range(N_HEADS):
        base = h * HEAD_W
        k_ref[0, h, :, :LANES] = kv[:, base:base + LANES].astype(BF16)
        k_ref[0, h, :, LANES:] = kr
        v_ref[0, h] = kv[:, base + LANES:base + HEAD_W].astype(BF16)

    ab = _dot(u, wa_ref[:, LAT_W:])
    c_ref[0] = ab[:, :D_CONV] * jax.nn.sigmoid(ab[:, D_CONV:])


def _project(x, tab, gmix, wa, gq, wq, gkv, wkv, tm):
    B, L, _ = x.shape
    const = lambda b, i: (0, 0)
    return pl.pallas_call(
        _proj_kernel,
        grid=(B, L // tm),
        in_specs=[
            pl.BlockSpec((1, tm, D_MODEL), lambda b, i: (b, i, 0)),
            pl.BlockSpec((tm, LANES), lambda b, i: (i, 0)),
            pl.BlockSpec((1, D_MODEL), const),
            pl.BlockSpec(wa.shape, const),
            pl.BlockSpec((1, Q_LORA), const),
            pl.BlockSpec(wq.shape, const),
            pl.BlockSpec((1, KV_LORA), const),
            pl.BlockSpec(wkv.shape, const),
        ],
        out_specs=[
            pl.BlockSpec((1, N_HEADS, tm, HEAD_W), lambda b, i: (b, 0, i, 0)),
            pl.BlockSpec((1, N_HEADS, tm, HEAD_W), lambda b, i: (b, 0, i, 0)),
            pl.BlockSpec((1, N_HEADS, tm, V_DIM), lambda b, i: (b, 0, i, 0)),
            pl.BlockSpec((1, tm, D_CONV), lambda b, i: (b, i, 0)),
        ],
        out_shape=[
            jax.ShapeDtypeStruct((B, N_HEADS, L, HEAD_W), BF16),
            jax.ShapeDtypeStruct((B, N_HEADS, L, HEAD_W), BF16),
            jax.ShapeDtypeStruct((B, N_HEADS, L, V_DIM), BF16),
            jax.ShapeDtypeStruct((B, L, D_CONV), F32),
        ],
        compiler_params=pltpu.CompilerParams(
            dimension_semantics=("arbitrary", "arbitrary"), vmem_limit_bytes=VMEM_LIMIT),
        name="proj",
    )(x, tab, gmix, wa, gq, wq, gkv, wkv)


def _lane_groups(a):
    return [a[:, g * LANES:(g + 1) * LANES] for g in range(a.shape[1] // LANES)]


def _attn_kernel(q_ref, k_ref, v_ref, km_ref, vm_ref, o_ref, s_scr, p_scr, *, seq, tk):
    q = q_ref[0, 0]
    n_chunks = seq // tk

    sm = _dot_nt(q, km_ref[0, 0])
    lane = lax.broadcasted_iota(jnp.int32, sm.shape, 1)
    sm = jnp.where(lane < N_META, sm, NEG_BIG)
    s_scr[:, seq:] = sm
    m_acc = sm
    for j in range(n_chunks):
        s = _dot_nt(q, k_ref[0, 0, j * tk:(j + 1) * tk, :])
        s_scr[:, j * tk:(j + 1) * tk] = s
        m_acc = functools.reduce(jnp.maximum, _lane_groups(s), m_acc)
    m = jnp.max(m_acc, axis=-1, keepdims=True)

    l_acc = jnp.zeros_like(m_acc)
    for g in range((seq + META_ROWS) // LANES):
        cols = slice(g * LANES, (g + 1) * LANES)
        p = jnp.exp2(s_scr[:, cols] - m)
        l_acc = l_acc + p
        p_scr[:, cols] = p.astype(BF16)
    l = jnp.sum(l_acc, axis=-1, keepdims=True)

    acc = _dot(p_scr[:, :seq], v_ref[0, 0]) + _dot(p_scr[:, seq:], vm_ref[0, 0])
    o_ref[0] = (acc / l).astype(BF16)


def _attention(q, k, v, k_meta, v_meta, tq, tk):
    B, H, L, _ = q.shape
    kern = functools.partial(_attn_kernel, seq=L, tk=tk)
    return pl.pallas_call(
        kern,
        grid=(B, H, L // tq),
        in_specs=[
            pl.BlockSpec((1, 1, tq, HEAD_W), lambda b, h, i: (b, h, i, 0)),
            pl.BlockSpec((1, 1, L, HEAD_W), lambda b, h, i: (b, h, 0, 0)),
            pl.BlockSpec((1, 1, L, V_DIM), lambda b, h, i: (b, h, 0, 0)),
            pl.BlockSpec((1, 1, META_ROWS, HEAD_W), lambda b, h, i: (0, h, 0, 0)),
            pl.BlockSpec((1, 1, META_ROWS, V_DIM), lambda b, h, i: (0, h, 0, 0)),
        ],
        out_specs=pl.BlockSpec((1, tq, V_DIM), lambda b, h, i: (b, i, h)),
        out_shape=jax.ShapeDtypeStruct((B, L, H * V_DIM), BF16),
        scratch_shapes=[
            pltpu.VMEM((tq, L + META_ROWS), F32),
            pltpu.VMEM((tq, L + META_ROWS), BF16),
        ],
        compiler_params=pltpu.CompilerParams(
            dimension_semantics=("arbitrary", "arbitrary", "arbitrary"), vmem_limit_bytes=VMEM_LIMIT),
        name="attn",
    )(q, k, v, k_meta, v_meta)


def _mix_kernel(x_ref, o_ref, c_ref, cprev_ref, cnext_ref, cmeta_ref,
                gmix_ref, wg_ref, bg_ref, wo_ref, cw_ref, cb_ref, lng_ref, lnb_ref, wco_ref, wout_ref,
                h_ref, ext_scr, conv_scr, *, tm, rows):
    i = pl.program_id(1)
    last = pl.num_programs(1) - 1

    ext_scr[:HALO] = jnp.where(i == 0, cmeta_ref[0], cprev_ref[0])
    ext_scr[HALO:HALO + tm] = c_ref[0]
    ext_scr[HALO + tm:] = jnp.where(i == last, 0.0, cnext_ref[0])

    off = HALO - CONV_PAD
    for r in range(tm // rows):
        for cb in range(D_CONV // LANES):
            cols = slice(cb * LANES, (cb + 1) * LANES)
            acc = jnp.broadcast_to(cb_ref[:, cols], (rows, LANES))
            for k in range(CONV_WIDTH):
                acc = acc + ext_scr[r * rows + off + k:r * rows + off + k + rows, cols] * cw_ref[k:k + 1, cols]
            conv_scr[r * rows:(r + 1) * rows, cols] = acc

    cv = conv_scr[...]
    mu = jnp.mean(cv, axis=-1, keepdims=True)
    var = jnp.mean(jnp.square(cv - mu), axis=-1, keepdims=True)
    y = (cv - mu) * lax.rsqrt(var + EPS) * lng_ref[...] + lnb_ref[...]
    conv = _dot((y * jax.nn.sigmoid(y)).astype(BF16), wco_ref[...])

    x = x_ref[0]
    u = _rms_f32(x, gmix_ref[...]).astype(BF16)
    gates = jax.nn.sigmoid(_dot(u, wg_ref[...]) + bg_ref[...])
    attn = _dot(o_ref[0], wo_ref[...])
    merged = gates[:, :D_MODEL] * attn + gates[:, D_MODEL:] * conv
    h_ref[0] = x + _dot(merged.astype(BF16), wout_ref[...])


def _mix(x, o, c, c_meta, gmix, wg, bg, wo, cw, cb, lng, lnb, wco, wout, tm, rows):
    B, L, _ = x.shape
    hb = tm // HALO
    n_halo = L // HALO
    const = lambda b, i: (0, 0)
    row = lambda b, i: (b, i, 0)
    kern = functools.partial(_mix_kernel, tm=tm, rows=rows)
    return pl.pallas_call(
        kern,
        grid=(B, L // tm),
        in_specs=[
            pl.BlockSpec((1, tm, D_MODEL), row),
            pl.BlockSpec((1, tm, D_MODEL), row),
            pl.BlockSpec((1, tm, D_CONV), row),
            pl.BlockSpec((1, HALO, D_CONV), lambda b, i: (b, jnp.maximum(i * hb - 1, 0), 0)),
            pl.BlockSpec((1, HALO, D_CONV), lambda b, i: (b, jnp.minimum((i + 1) * hb, n_halo - 1), 0)),
            pl.BlockSpec((1, HALO, D_CONV), lambda b, i: (0, 0, 0)),
            pl.BlockSpec((1, D_MODEL), const),
            pl.BlockSpec(wg.shape, const),
            pl.BlockSpec((1, N_GATE), const),
            pl.BlockSpec(wo.shape, const),
            pl.BlockSpec(cw.shape, const),
            pl.BlockSpec((1, D_CONV), const),
            pl.BlockSpec((1, D_CONV), const),
            pl.BlockSpec((1, D_CONV), const),
            pl.BlockSpec(wco.shape, const),
            pl.BlockSpec(wout.shape, const),
        ],
        out_specs=pl.BlockSpec((1, tm, D_MODEL), row),
        out_shape=jax.ShapeDtypeStruct((B, L, D_MODEL), F32),
        scratch_shapes=[
            pltpu.VMEM((tm + 2 * HALO, D_CONV), F32),
            pltpu.VMEM((tm, D_CONV), F32),
        ],
        compiler_params=pltpu.CompilerParams(
            dimension_semantics=("arbitrary", "arbitrary"), vmem_limit_bytes=VMEM_LIMIT),
        name="mix",
    )(x, o, c, c, c, c_meta, gmix, wg, bg, wo, cw, cb, lng, lnb, wco, wout)


def _mlp_kernel(h_ref, gmlp_ref, w1_ref, w2_ref, gfin_ref, y_ref):
    h = h_ref[0]
    u = _rms_f32(h, gmlp_ref[...]).astype(BF16)
    hid = jnp.square(jnp.maximum(_dot(u, w1_ref[...]), 0.0)).astype(BF16)
    h2 = h + _dot(hid, w2_ref[...])
    y_ref[0] = _rms_f32(h2, gfin_ref[...])


def _mlp(h, gmlp, w1, w2, gfin, tm):
    B, L, _ = h.shape
    const = lambda b, i: (0, 0)
    row = lambda b, i: (b, i, 0)
    return pl.pallas_call(
        _mlp_kernel,
        grid=(B, L // tm),
        in_specs=[
            pl.BlockSpec((1, tm, D_MODEL), row),
            pl.BlockSpec((1, D_MODEL), const),
            pl.BlockSpec(w1.shape, const),
            pl.BlockSpec(w2.shape, const),
            pl.BlockSpec((1, D_MODEL), const),
        ],
        out_specs=pl.BlockSpec((1, tm, D_MODEL), row),
        out_shape=jax.ShapeDtypeStruct((B, L, D_MODEL), F32),
        compiler_params=pltpu.CompilerParams(
            dimension_semantics=("arbitrary", "arbitrary"), vmem_limit_bytes=VMEM_LIMIT),
        name="mlp",
    )(h, gmlp, w1, w2, gfin)


def _rotate_half_cols(w):
    half = ROPE_DIM // 2
    return jnp.concatenate([-w[..., half:], w[..., :half]], axis=-1)


def _rope_table(n):
    inv = ROPE_THETA ** (-jnp.arange(0, ROPE_DIM, 2, dtype=F32) / ROPE_DIM)
    ang = jnp.arange(n, dtype=F32)[:, None] * inv[None, :]
    cos, sin = jnp.cos(ang), jnp.sin(ang)
    return jnp.concatenate([cos, cos, sin, sin], axis=-1)


def kernel(x_prompt, x_sample, meta_tokens, norm_mix, w_in, b_gate, norm_q, w_uq, norm_kv, w_ukv,
           w_o_attn, conv_w, conv_b, conv_ln_g, conv_ln_b, w_conv_out, w_out, norm_mlp, w_mlp_in,
           w_mlp_out, norm_final):
    assert w_in.shape[0] == 1, "single-layer block"
    row2 = lambda a: a.reshape(1, -1).astype(F32)

    w_in0 = w_in[0]
    o_kr = Q_LORA + KV_LORA
    o_conv = o_kr + ROPE_DIM
    o_gate = o_conv + 2 * D_CONV
    w_kr = w_in0[:, o_kr:o_conv]
    wa = jnp.concatenate(
        [w_in0[:, :o_kr], w_kr, _rotate_half_cols(w_kr), w_in0[:, o_conv:o_gate]], axis=1).astype(BF16)
    wg = w_in0[:, o_gate:].astype(BF16)
    wq3 = w_uq[0].reshape(Q_LORA, N_HEADS, NOPE_DIM + ROPE_DIM)
    wq_r = wq3[..., NOPE_DIM:]
    wq = jnp.concatenate([wq3[..., :NOPE_DIM], wq_r, _rotate_half_cols(wq_r)], axis=-1)
    wq = wq.reshape(Q_LORA, N_HEADS * HEAD_W).astype(BF16)
    wkv = w_ukv[0].astype(BF16)
    wo = w_o_attn[0].astype(BF16)
    wco = w_conv_out[0].astype(BF16)
    wout = w_out[0].astype(BF16)
    w1 = w_mlp_in[0].astype(BF16)
    w2 = w_mlp_out[0].astype(BF16)
    cw = jnp.pad(conv_w[0].astype(F32), ((0, 1), (0, 0)))
    gmix, gq, gkv = row2(norm_mix[0]), row2(norm_q[0]), row2(norm_kv[0])
    bg, cb = row2(b_gate[0]), row2(conv_b[0])
    lng, lnb = row2(conv_ln_g[0]), row2(conv_ln_b[0])
    gmlp, gfin = row2(norm_mlp[0]), row2(norm_final)

    max_len = max(x_prompt.shape[1], x_sample.shape[1], META_ROWS)
    tab = _rope_table(N_META + max_len)

    meta = jnp.pad(meta_tokens.astype(F32), ((0, META_ROWS - N_META), (0, 0)))[None]
    _, k_meta, v_meta, c_meta = _project(meta, tab[:META_ROWS], gmix, wa, gq, wq, gkv, wkv, META_ROWS)

    def encode(x):
        L = x.shape[1]
        q, k, v, c = _project(x, tab[N_META:N_META + L], gmix, wa, gq, wq, gkv, wkv, 256)
        o = _attention(q, k, v, k_meta, v_meta, 256, 512)
        h = _mix(x, o, c, c_meta, gmix, wg, bg, wo, cw, cb, lng, lnb, wco, wout, 256, 64)
        return _mlp(h, gmlp, w1, w2, gfin, 256)

    return encode(x_prompt), encode(x_sample)
```

```python
import functools
import math

import jax
import jax.numpy as jnp
from jax import lax
from jax.experimental import pallas as pl
from jax.experimental.pallas import tpu as pltpu

D_MODEL = 1024
N_META = 16
N_HEADS = 8
NOPE_DIM = 128
ROPE_DIM = 64
V_DIM = 128
Q_LORA = 512
KV_LORA = 256
ROPE_THETA = 10000.0
D_CONV = D_MODEL
CONV_WIDTH = 31
CONV_PAD = CONV_WIDTH // 2
D_FF = 4 * D_MODEL
N_GATE = 2 * D_MODEL
EPS = 1e-6

LANES = 128
HEAD_W = 2 * LANES
META_ROWS = LANES
HALO = 16
LAT_W = Q_LORA + KV_LORA + LANES
N_CBLK = D_CONV // LANES
Q_SCALE = math.log2(math.e) / math.sqrt(NOPE_DIM + ROPE_DIM)
NEG_BIG = -1e30
VMEM_LIMIT = 56 * 1024 * 1024
ATTN_SUB = 256

F32 = jnp.float32
BF16 = jnp.bfloat16


def _rms_f32(x, g):
    return x * lax.rsqrt(jnp.mean(x * x, axis=-1, keepdims=True) + EPS) * g


def _dot(a, b):
    return jnp.dot(a, b, preferred_element_type=F32)


def _dot_nt(a, b):
    return lax.dot_general(a, b, (((1,), (1,)), ((), ())), preferred_element_type=F32)


def _rope_pair_sum(t):
    return t + pltpu.roll(t, ROPE_DIM, 1)


def _resident(shape):
    return pl.BlockSpec(shape, lambda *_: (0,) * len(shape), pipeline_mode=pl.Buffered(1))


def _proj_kernel(x_ref, tab_ref, gmix_ref, wa_ref, gq_ref, wq_ref, gkv_ref, wkv_ref,
                 q_ref, k_ref, v_ref, c_ref):
    x = x_ref[0]
    u = _rms_f32(x, gmix_ref[...]).astype(BF16)
    tab = tab_ref[...]
    lat = _dot(u, wa_ref[:, :LAT_W])

    rq = _rms_f32(lat[:, :Q_LORA], gq_ref[...]).astype(BF16)
    qa = _dot(rq, wq_ref[...])
    for h in range(N_HEADS):
        base = h * HEAD_W
        qn = qa[:, base:base + LANES] * Q_SCALE
        qr = _rope_pair_sum(qa[:, base + LANES:base + HEAD_W] * tab) * Q_SCALE
        q_ref[0, h, :, :LANES] = qn.astype(BF16)
        q_ref[0, h, :, LANES:] = qr.astype(BF16)

    kr = _rope_pair_sum(lat[:, Q_LORA + KV_LORA:LAT_W] * tab)
    lane = lax.broadcasted_iota(jnp.int32, kr.shape, 1)
    kr = jnp.where(lane < ROPE_DIM, kr, 0.0).astype(BF16)

    rkv = _rms_f32(lat[:, Q_LORA:Q_LORA + KV_LORA], gkv_ref[...]).astype(BF16)
    kv = _dot(rkv, wkv_ref[...])
    for h in range(N_HEADS):
        base = h * HEAD_W
        k_ref[0, h, :, :LANES] = kv[:, base:base + LANES].astype(BF16)
        k_ref[0, h, :, LANES:] = kr
        v_ref[0, h] = kv[:, base + LANES:base + HEAD_W].astype(BF16)

    ab = _dot(u, wa_ref[:, LAT_W:])
    c_ref[0] = ab[:, :D_CONV] * jax.nn.sigmoid(ab[:, D_CONV:])


def _project(x, tab, gmix, wa, gq, wq, gkv, wkv, tm):
    B, L, _ = x.shape
    return pl.pallas_call(
        _proj_kernel,
        grid=(B, L // tm),
        in_specs=[
            pl.BlockSpec((1, tm, D_MODEL), lambda b, i: (b, i, 0)),
            pl.BlockSpec((tm, LANES), lambda b, i: (i, 0)),
            _resident((1, D_MODEL)),
            _resident(wa.shape),
            _resident((1, Q_LORA)),
            _resident(wq.shape),
            _resident((1, KV_LORA)),
            _resident(wkv.shape),
        ],
        out_specs=[
            pl.BlockSpec((1, N_HEADS, tm, HEAD_W), lambda b, i: (b, 0, i, 0)),
            pl.BlockSpec((1, N_HEADS, tm, HEAD_W), lambda b, i: (b, 0, i, 0)),
            pl.BlockSpec((1, N_HEADS, tm, V_DIM), lambda b, i: (b, 0, i, 0)),
            pl.BlockSpec((1, tm, D_CONV), lambda b, i: (b, i, 0)),
        ],
        out_shape=[
            jax.ShapeDtypeStruct((B, N_HEADS, L, HEAD_W), BF16),
            jax.ShapeDtypeStruct((B, N_HEADS, L, HEAD_W), BF16),
            jax.ShapeDtypeStruct((B, N_HEADS, L, V_DIM), BF16),
            jax.ShapeDtypeStruct((B, L, D_CONV), F32),
        ],
        compiler_params=pltpu.CompilerParams(
            dimension_semantics=("arbitrary", "arbitrary"), vmem_limit_bytes=VMEM_LIMIT),
        name="proj",
    )(x, tab, gmix, wa, gq, wq, gkv, wkv)


def _lane_groups(a):
    return [a[:, g * LANES:(g + 1) * LANES] for g in range(a.shape[1] // LANES)]


def _attn_kernel(q_ref, k_ref, v_ref, km_ref, vm_ref, o_ref, s_scr, p_scr, *, seq, tk, n_sub):
    n_chunks = seq // tk
    n_groups = (seq + META_ROWS) // LANES
    groups_per_chunk = -(-n_groups // (n_chunks + 1))
    lane = lax.broadcasted_iota(jnp.int32, (ATTN_SUB, LANES), 1)

    def q_rows(t):
        return q_ref[0, 0, t * ATTN_SUB:(t + 1) * ATTN_SUB, :]

    def score_meta(t):
        sm = jnp.where(lane < N_META, _dot_nt(q_rows(t), km_ref[0, 0]), NEG_BIG)
        s_scr[t % 2, :, seq:] = sm
        return sm

    def score_chunk(t, j, m_acc):
        s = _dot_nt(q_rows(t), k_ref[0, 0, j * tk:(j + 1) * tk, :])
        s_scr[t % 2, :, j * tk:(j + 1) * tk] = s
        return functools.reduce(jnp.maximum, _lane_groups(s), m_acc)

    def prob_group(t, g, m, l_acc):
        cols = slice(g * LANES, (g + 1) * LANES)
        p = jnp.exp2(s_scr[t % 2, :, cols] - m)
        p_scr[t % 2, :, cols] = p.astype(BF16)
        return l_acc + p

    def weighted_values(t, l_acc):
        l = jnp.sum(l_acc, axis=-1, keepdims=True)
        acc = _dot(p_scr[t % 2, :, :seq], v_ref[0, 0]) + _dot(p_scr[t % 2, :, seq:], vm_ref[0, 0])
        o_ref[0, t * ATTN_SUB:(t + 1) * ATTN_SUB, :] = (acc / l).astype(BF16)

    row_max = {}
    row_sum = {}
    for stage in range(n_sub + 2):
        t_score, t_prob, t_out = stage, stage - 1, stage - 2
        if 0 <= t_out < n_sub:
            weighted_values(t_out, row_sum.pop(t_out))
        do_score = t_score < n_sub
        do_prob = 0 <= t_prob < n_sub
        m_acc = score_meta(t_score) if do_score else None
        l_acc = jnp.zeros((ATTN_SUB, LANES), F32) if do_prob else None
        g = 0
        for j in range(n_chunks + 1):
            if do_score and j < n_chunks:
                m_acc = score_chunk(t_score, j, m_acc)
            if do_prob:
                for _ in range(groups_per_chunk):
                    if g < n_groups:
                        l_acc = prob_group(t_prob, g, row_max[t_prob], l_acc)
                        g += 1
        if do_score:
            row_max[t_score] = jnp.max(m_acc, axis=-1, keepdims=True)
        if do_prob:
            row_sum[t_prob] = l_acc


def _attention(q, k, v, k_meta, v_meta, tq, tk):
    B, H, L, _ = q.shape
    kern = functools.partial(_attn_kernel, seq=L, tk=tk, n_sub=tq // ATTN_SUB)
    return pl.pallas_call(
        kern,
        grid=(B, H, L // tq),
        in_specs=[
            pl.BlockSpec((1, 1, tq, HEAD_W), lambda b, h, i: (b, h, i, 0)),
            pl.BlockSpec((1, 1, L, HEAD_W), lambda b, h, i: (b, h, 0, 0)),
            pl.BlockSpec((1, 1, L, V_DIM), lambda b, h, i: (b, h, 0, 0)),
            pl.BlockSpec((1, 1, META_ROWS, HEAD_W), lambda b, h, i: (0, h, 0, 0)),
            pl.BlockSpec((1, 1, META_ROWS, V_DIM), lambda b, h, i: (0, h, 0, 0)),
        ],
        out_specs=pl.BlockSpec((1, tq, V_DIM), lambda b, h, i: (b, i, h)),
        out_shape=jax.ShapeDtypeStruct((B, L, H * V_DIM), BF16),
        scratch_shapes=[
            pltpu.VMEM((2, ATTN_SUB, L + META_ROWS), F32),
            pltpu.VMEM((2, ATTN_SUB, L + META_ROWS), BF16),
        ],
        compiler_params=pltpu.CompilerParams(
            dimension_semantics=("arbitrary", "arbitrary", "arbitrary"), vmem_limit_bytes=VMEM_LIMIT),
        name="attn",
    )(q, k, v, k_meta, v_meta)


def _mix_kernel(x_ref, o_ref, c_ref, cprev_ref, cnext_ref, cmeta_ref,
                gmix_ref, wg_ref, bg_ref, wo_ref, cw_ref, cb_ref, lng_ref, lnb_ref, wco_ref, wout_ref,
                h_ref, ext_scr, conv_scr, *, tm, rows):
    i = pl.program_id(1)
    last = pl.num_programs(1) - 1

    prev = jnp.where(i == 0, cmeta_ref[0], cprev_ref[0])
    nxt = jnp.where(i == last, 0.0, cnext_ref[0])
    for cb in range(N_CBLK):
        cols = slice(cb * LANES, (cb + 1) * LANES)
        ext_scr[cb, :HALO, :] = prev[:, cols]
        ext_scr[cb, HALO:HALO + tm, :] = c_ref[0, :, cols]
        ext_scr[cb, HALO + tm:, :] = nxt[:, cols]

    off = HALO - CONV_PAD
    for r in range(tm // rows):
        for cb in range(N_CBLK):
            cols = slice(cb * LANES, (cb + 1) * LANES)
            acc = jnp.broadcast_to(cb_ref[:, cols], (rows, LANES))
            for k in range(CONV_WIDTH):
                start = r * rows + off + k
                acc = acc + ext_scr[cb, start:start + rows, :] * cw_ref[k:k + 1, cols]
            conv_scr[r * rows:(r + 1) * rows, cols] = acc

    cv = conv_scr[...]
    mu = jnp.mean(cv, axis=-1, keepdims=True)
    var = jnp.mean(jnp.square(cv - mu), axis=-1, keepdims=True)
    y = (cv - mu) * lax.rsqrt(var + EPS) * lng_ref[...] + lnb_ref[...]
    conv = _dot((y * jax.nn.sigmoid(y)).astype(BF16), wco_ref[...])

    x = x_ref[0]
    u = _rms_f32(x, gmix_ref[...]).astype(BF16)
    gates = jax.nn.sigmoid(_dot(u, wg_ref[...]) + bg_ref[...])
    attn = _dot(o_ref[0], wo_ref[...])
    merged = gates[:, :D_MODEL] * attn + gates[:, D_MODEL:] * conv
    h_ref[0] = x + _dot(merged.astype(BF16), wout_ref[...])


def _mix(x, o, c, c_meta, gmix, wg, bg, wo, cw, cb, lng, lnb, wco, wout, tm, rows):
    B, L, _ = x.shape
    hb = tm // HALO
    n_halo = L // HALO
    row = lambda b, i: (b, i, 0)
    kern = functools.partial(_mix_kernel, tm=tm, rows=rows)
    return pl.pallas_call(
        kern,
        grid=(B, L // tm),
        in_specs=[
            pl.BlockSpec((1, tm, D_MODEL), row),
            pl.BlockSpec((1, tm, D_MODEL), row),
            pl.BlockSpec((1, tm, D_CONV), row),
            pl.BlockSpec((1, HALO, D_CONV), lambda b, i: (b, jnp.maximum(i * hb - 1, 0), 0)),
            pl.BlockSpec((1, HALO, D_CONV), lambda b, i: (b, jnp.minimum((i + 1) * hb, n_halo - 1), 0)),
            pl.BlockSpec((1, HALO, D_CONV), lambda b, i: (0, 0, 0)),
            _resident((1, D_MODEL)),
            _resident(wg.shape),
            _resident((1, N_GATE)),
            _resident(wo.shape),
            _resident(cw.shape),
            _resident((1, D_CONV)),
            _resident((1, D_CONV)),
            _resident((1, D_CONV)),
            _resident(wco.shape),
            _resident(wout.shape),
        ],
        out_specs=pl.BlockSpec((1, tm, D_MODEL), row),
        out_shape=jax.ShapeDtypeStruct((B, L, D_MODEL), F32),
        scratch_shapes=[
            pltpu.VMEM((N_CBLK, tm + 2 * HALO, LANES), F32),
            pltpu.VMEM((tm, D_CONV), F32),
        ],
        compiler_params=pltpu.CompilerParams(
            dimension_semantics=("arbitrary", "arbitrary"), vmem_limit_bytes=VMEM_LIMIT),
        name="mix",
    )(x, o, c, c, c, c_meta, gmix, wg, bg, wo, cw, cb, lng, lnb, wco, wout)


def _mlp_kernel(h_ref, gmlp_ref, w1_ref, w2_ref, gfin_ref, y_ref):
    h = h_ref[0]
    u = _rms_f32(h, gmlp_ref[...]).astype(BF16)
    hid = jnp.square(jnp.maximum(_dot(u, w1_ref[...]), 0.0)).astype(BF16)
    h2 = h + _dot(hid, w2_ref[...])
    y_ref[0] = _rms_f32(h2, gfin_ref[...])


def _mlp(h, gmlp, w1, w2, gfin, tm):
    B, L, _ = h.shape
    row = lambda b, i: (b, i, 0)
    return pl.pallas_call(
        _mlp_kernel,
        grid=(B, L // tm),
        in_specs=[
            pl.BlockSpec((1, tm, D_MODEL), row),
            _resident((1, D_MODEL)),
            _resident(w1.shape),
            _resident(w2.shape),
            _resident((1, D_MODEL)),
        ],
        out_specs=pl.BlockSpec((1, tm, D_MODEL), row),
        out_shape=jax.ShapeDtypeStruct((B, L, D_MODEL), F32),
        compiler_params=pltpu.CompilerParams(
            dimension_semantics=("arbitrary", "arbitrary"), vmem_limit_bytes=VMEM_LIMIT),
        name="mlp",
    )(h, gmlp, w1, w2, gfin)


def _rotate_half_cols(w):
    half = ROPE_DIM // 2
    return jnp.concatenate([-w[..., half:], w[..., :half]], axis=-1)


def _rope_table(n):
    inv = ROPE_THETA ** (-jnp.arange(0, ROPE_DIM, 2, dtype=F32) / ROPE_DIM)
    ang = jnp.arange(n, dtype=F32)[:, None] * inv[None, :]
    cos, sin = jnp.cos(ang), jnp.sin(ang)
    return jnp.concatenate([cos, cos, sin, sin], axis=-1)


def kernel(x_prompt, x_sample, meta_tokens, norm_mix, w_in, b_gate, norm_q, w_uq, norm_kv, w_ukv,
           w_o_attn, conv_w, conv_b, conv_ln_g, conv_ln_b, w_conv_out, w_out, norm_mlp, w_mlp_in,
           w_mlp_out, norm_final):
    assert w_in.shape[0] == 1, "single-layer block"
    row2 = lambda a: a.reshape(1, -1).astype(F32)

    w_in0 = w_in[0]
    o_kr = Q_LORA + KV_LORA
    o_conv = o_kr + ROPE_DIM
    o_gate = o_conv + 2 * D_CONV
    w_kr = w_in0[:, o_kr:o_conv]
    wa = jnp.concatenate(
        [w_in0[:, :o_kr], w_kr, _rotate_half_cols(w_kr), w_in0[:, o_conv:o_gate]], axis=1).astype(BF16)
    wg = w_in0[:, o_gate:].astype(BF16)
    wq3 = w_uq[0].reshape(Q_LORA, N_HEADS, NOPE_DIM + ROPE_DIM)
    wq_r = wq3[..., NOPE_DIM:]
    wq = jnp.concatenate([wq3[..., :NOPE_DIM], wq_r, _rotate_half_cols(wq_r)], axis=-1)
    wq = wq.reshape(Q_LORA, N_HEADS * HEAD_W).astype(BF16)
    wkv = w_ukv[0].astype(BF16)
    wo = w_o_attn[0].astype(BF16)
    wco = w_conv_out[0].astype(BF16)
    wout = w_out[0].astype(BF16)
    w1 = w_mlp_in[0].astype(BF16)
    w2 = w_mlp_out[0].astype(BF16)
    cw = jnp.pad(conv_w[0].astype(F32), ((0, 1), (0, 0)))
    gmix, gq, gkv = row2(norm_mix[0]), row2(norm_q[0]), row2(norm_kv[0])
    bg, cb = row2(b_gate[0]), row2(conv_b[0])
    lng, lnb = row2(conv_ln_g[0]), row2(conv_ln_b[0])
    gmlp, gfin = row2(norm_mlp[0]), row2(norm_final)

    max_len = max(x_prompt.shape[1], x_sample.shape[1], META_ROWS)
    tab = _rope_table(N_META + max_len)

    meta = jnp.pad(meta_tokens.astype(F32), ((0, META_ROWS - N_META), (0, 0)))[None]
    _, k_meta, v_meta, c_meta = _project(meta, tab[:META_ROWS], gmix, wa, gq, wq, gkv, wkv, META_ROWS)

    def encode(x):
        L = x.shape[1]
        q, k, v, c = _project(x, tab[N_META:N_META + L], gmix, wa, gq, wq, gkv, wkv, 512)
        o = _attention(q, k, v, k_meta, v_meta, 1024, 512)
        h = _mix(x, o, c, c_meta, gmix, wg, bg, wo, cw, cb, lng, lnb, wco, wout, 256, 64)
        return _mlp(h, gmlp, w1, w2, gfin, 512)

    return encode(x_prompt), encode(x_sample)
```

```python
import functools
import math

import jax
import jax.numpy as jnp
from jax import lax
from jax.experimental import pallas as pl
from jax.experimental.pallas import tpu as pltpu

D_MODEL = 1024
N_META = 16
N_HEADS = 8
NOPE_DIM = 128
ROPE_DIM = 64
V_DIM = 128
Q_LORA = 512
KV_LORA = 256
ROPE_THETA = 10000.0
D_CONV = D_MODEL
CONV_WIDTH = 31
CONV_PAD = CONV_WIDTH // 2
D_FF = 4 * D_MODEL
N_GATE = 2 * D_MODEL
EPS = 1e-6

LANES = 128
HEAD_W = 2 * LANES
META_ROWS = LANES
HALO = 16
LAT_W = Q_LORA + KV_LORA + LANES
N_CBLK = D_CONV // LANES
Q_SCALE = math.log2(math.e) / math.sqrt(NOPE_DIM + ROPE_DIM)
NEG_BIG = -1e30
VMEM_LIMIT = 56 * 1024 * 1024
ATTN_SUB = 256
MIX_SLICE = 512

F32 = jnp.float32
BF16 = jnp.bfloat16


def _rms_f32(x, g):
    return x * lax.rsqrt(jnp.mean(x * x, axis=-1, keepdims=True) + EPS) * g


def _dot(a, b):
    return jnp.dot(a, b, preferred_element_type=F32)


def _dot_nt(a, b):
    return lax.dot_general(a, b, (((1,), (1,)), ((), ())), preferred_element_type=F32)


def _rope_pair_sum(t):
    return t + pltpu.roll(t, ROPE_DIM, 1)


def _or_bits(xs):
    groups = [pltpu.bitcast(x, jnp.int32)[r:r + 8] for x in xs for r in range(0, x.shape[0], 8)]
    return functools.reduce(jnp.bitwise_or, groups)


def _dep_zero_row(xs, zmask):
    return pltpu.bitcast(_or_bits(xs) & zmask, F32)[:1].astype(BF16)


def _resident(shape):
    return pl.BlockSpec(shape, lambda *_: (0,) * len(shape), pipeline_mode=pl.Buffered(1))


def _proj_kernel(x_ref, tab_ref, gmix_ref, wa_ref, gq_ref, wq_ref, gkv_ref, wkv_ref,
                 q_ref, k_ref, v_ref, c_ref):
    x = x_ref[0]
    u = _rms_f32(x, gmix_ref[...]).astype(BF16)
    tab = tab_ref[...]
    lat = _dot(u, wa_ref[:, :LAT_W])

    rq = _rms_f32(lat[:, :Q_LORA], gq_ref[...]).astype(BF16)
    qa = _dot(rq, wq_ref[...])
    for h in range(N_HEADS):
        base = h * HEAD_W
        qn = qa[:, base:base + LANES] * Q_SCALE
        qr = _rope_pair_sum(qa[:, base + LANES:base + HEAD_W] * tab) * Q_SCALE
        q_ref[0, h, :, :LANES] = qn.astype(BF16)
        q_ref[0, h, :, LANES:] = qr.astype(BF16)

    kr = _rope_pair_sum(lat[:, Q_LORA + KV_LORA:LAT_W] * tab)
    lane = lax.broadcasted_iota(jnp.int32, kr.shape, 1)
    kr = jnp.where(lane < ROPE_DIM, kr, 0.0).astype(BF16)

    rkv = _rms_f32(lat[:, Q_LORA:Q_LORA + KV_LORA], gkv_ref[...]).astype(BF16)
    kv = _dot(rkv, wkv_ref[...])
    for h in range(N_HEADS):
        base = h * HEAD_W
        k_ref[0, h, :, :LANES] = kv[:, base:base + LANES].astype(BF16)
        k_ref[0, h, :, LANES:] = kr
        v_ref[0, h] = kv[:, base + LANES:base + HEAD_W].astype(BF16)

    ab = _dot(u, wa_ref[:, LAT_W:])
    c_ref[0] = ab[:, :D_CONV] * jax.nn.sigmoid(ab[:, D_CONV:])


def _project(x, tab, gmix, wa, gq, wq, gkv, wkv, tm):
    B, L, _ = x.shape
    return pl.pallas_call(
        _proj_kernel,
        grid=(B, L // tm),
        in_specs=[
            pl.BlockSpec((1, tm, D_MODEL), lambda b, i: (b, i, 0)),
            pl.BlockSpec((tm, LANES), lambda b, i: (i, 0)),
            _resident((1, D_MODEL)),
            _resident(wa.shape),
            _resident((1, Q_LORA)),
            _resident(wq.shape),
            _resident((1, KV_LORA)),
            _resident(wkv.shape),
        ],
        out_specs=[
            pl.BlockSpec((1, N_HEADS, tm, HEAD_W), lambda b, i: (b, 0, i, 0)),
            pl.BlockSpec((1, N_HEADS, tm, HEAD_W), lambda b, i: (b, 0, i, 0)),
            pl.BlockSpec((1, N_HEADS, tm, V_DIM), lambda b, i: (b, 0, i, 0)),
            pl.BlockSpec((1, tm, D_CONV), lambda b, i: (b, i, 0)),
        ],
        out_shape=[
            jax.ShapeDtypeStruct((B, N_HEADS, L, HEAD_W), BF16),
            jax.ShapeDtypeStruct((B, N_HEADS, L, HEAD_W), BF16),
            jax.ShapeDtypeStruct((B, N_HEADS, L, V_DIM), BF16),
            jax.ShapeDtypeStruct((B, L, D_CONV), F32),
        ],
        compiler_params=pltpu.CompilerParams(
            dimension_semantics=("arbitrary", "arbitrary"), vmem_limit_bytes=VMEM_LIMIT),
        name="proj",
    )(x, tab, gmix, wa, gq, wq, gkv, wkv)


def _lane_groups(a):
    return [a[:, g * LANES:(g + 1) * LANES] for g in range(a.shape[1] // LANES)]


def _attn_kernel(q_ref, k_ref, v_ref, km_ref, vm_ref, o_ref, s_scr, p_scr, *, seq, tk, n_sub):
    n_chunks = seq // tk
    n_groups = (seq + META_ROWS) // LANES
    groups_per_chunk = -(-n_groups // (n_chunks + 1))
    lane = lax.broadcasted_iota(jnp.int32, (ATTN_SUB, LANES), 1)

    def q_rows(t):
        return q_ref[0, 0, t * ATTN_SUB:(t + 1) * ATTN_SUB, :]

    def score_meta(t):
        sm = jnp.where(lane < N_META, _dot_nt(q_rows(t), km_ref[0, 0]), NEG_BIG)
        s_scr[t % 2, :, seq:] = sm
        return sm

    def score_chunk(t, j, m_acc):
        s = _dot_nt(q_rows(t), k_ref[0, 0, j * tk:(j + 1) * tk, :])
        s_scr[t % 2, :, j * tk:(j + 1) * tk] = s
        return functools.reduce(jnp.maximum, _lane_groups(s), m_acc)

    def prob_group(t, g, m, l_acc):
        cols = slice(g * LANES, (g + 1) * LANES)
        p = jnp.exp2(s_scr[t % 2, :, cols] - m)
        p_scr[t % 2, :, cols] = p.astype(BF16)
        return l_acc + p

    def weighted_values(t, l_acc):
        l = jnp.sum(l_acc, axis=-1, keepdims=True)
        acc = _dot(p_scr[t % 2, :, :seq], v_ref[0, 0]) + _dot(p_scr[t % 2, :, seq:], vm_ref[0, 0])
        o_ref[0, t * ATTN_SUB:(t + 1) * ATTN_SUB, :] = (acc / l).astype(BF16)

    row_max = {}
    row_sum = {}
    for stage in range(n_sub + 2):
        t_score, t_prob, t_out = stage, stage - 1, stage - 2
        if 0 <= t_out < n_sub:
            weighted_values(t_out, row_sum.pop(t_out))
        do_score = t_score < n_sub
        do_prob = 0 <= t_prob < n_sub
        m_acc = score_meta(t_score) if do_score else None
        l_acc = jnp.zeros((ATTN_SUB, LANES), F32) if do_prob else None
        g = 0
        for j in range(n_chunks + 1):
            if do_score and j < n_chunks:
                m_acc = score_chunk(t_score, j, m_acc)
            if do_prob:
                for _ in range(groups_per_chunk):
                    if g < n_groups:
                        l_acc = prob_group(t_prob, g, row_max[t_prob], l_acc)
                        g += 1
        if do_score:
            row_max[t_score] = jnp.max(m_acc, axis=-1, keepdims=True)
        if do_prob:
            row_sum[t_prob] = l_acc


def _attention(q, k, v, k_meta, v_meta, tq, tk):
    B, H, L, _ = q.shape
    kern = functools.partial(_attn_kernel, seq=L, tk=tk, n_sub=tq // ATTN_SUB)
    return pl.pallas_call(
        kern,
        grid=(B, H, L // tq),
        in_specs=[
            pl.BlockSpec((1, 1, tq, HEAD_W), lambda b, h, i: (b, h, i, 0)),
            pl.BlockSpec((1, 1, L, HEAD_W), lambda b, h, i: (b, h, 0, 0)),
            pl.BlockSpec((1, 1, L, V_DIM), lambda b, h, i: (b, h, 0, 0)),
            pl.BlockSpec((1, 1, META_ROWS, HEAD_W), lambda b, h, i: (0, h, 0, 0)),
            pl.BlockSpec((1, 1, META_ROWS, V_DIM), lambda b, h, i: (0, h, 0, 0)),
        ],
        out_specs=pl.BlockSpec((1, tq, V_DIM), lambda b, h, i: (b, i, h)),
        out_shape=jax.ShapeDtypeStruct((B, L, H * V_DIM), BF16),
        scratch_shapes=[
            pltpu.VMEM((2, ATTN_SUB, L + META_ROWS), F32),
            pltpu.VMEM((2, ATTN_SUB, L + META_ROWS), BF16),
        ],
        compiler_params=pltpu.CompilerParams(
            dimension_semantics=("arbitrary", "arbitrary", "arbitrary"), vmem_limit_bytes=VMEM_LIMIT),
        name="attn",
    )(q, k, v, k_meta, v_meta)


def _mix_kernel(x_ref, o_ref, c_ref, cprev_ref, cnext_ref, cmeta_ref,
                gmix_ref, wg_ref, bg_ref, wo_ref, cw_ref, cb_ref, lng_ref, lnb_ref, wco_ref, wout_ref,
                zmask_ref, h_ref, ext_scr, conv_scr, *, tm, rows):
    i = pl.program_id(1)
    last = pl.num_programs(1) - 1

    prev = jnp.where(i == 0, cmeta_ref[0], cprev_ref[0])
    nxt = jnp.where(i == last, 0.0, cnext_ref[0])
    for cb in range(N_CBLK):
        cols = slice(cb * LANES, (cb + 1) * LANES)
        ext_scr[cb, :HALO, :] = prev[:, cols]
        ext_scr[cb, HALO:HALO + tm, :] = c_ref[0, :, cols]
        ext_scr[cb, HALO + tm:, :] = nxt[:, cols]

    u = _rms_f32(x_ref[0], gmix_ref[...]).astype(BF16)
    gate_parts, attn_parts = [], []

    def gate_slice(n, dep):
        cols = slice(n * MIX_SLICE, (n + 1) * MIX_SLICE)
        lhs = u if dep is None else u + jnp.tile(dep, (1, D_MODEL // LANES))
        gate_parts.append(jax.nn.sigmoid(_dot(lhs, wg_ref[:, cols]) + bg_ref[:, cols]))

    def attn_slice(n, dep):
        cols = slice(n * MIX_SLICE, (n + 1) * MIX_SLICE)
        lhs = o_ref[0] if dep is None else o_ref[0] + jnp.tile(dep, (1, D_MODEL // LANES))
        attn_parts.append(_dot(lhs, wo_ref[:, cols]))

    side_work = [functools.partial(gate_slice, n) for n in range(N_GATE // MIX_SLICE)]
    side_work += [functools.partial(attn_slice, n) for n in range(D_MODEL // MIX_SLICE)]

    off = HALO - CONV_PAD
    units = [(r, cb) for r in range(tm // rows) for cb in range(N_CBLK)]
    every = len(units) // len(side_work)
    done = []
    for n, (r, cb) in enumerate(units):
        if n % every == 0 and side_work:
            side_work.pop(0)(_dep_zero_row(done, zmask_ref[...]) if done else None)
            done = []
        cols = slice(cb * LANES, (cb + 1) * LANES)
        acc = jnp.broadcast_to(cb_ref[:, cols], (rows, LANES))
        for k in range(CONV_WIDTH):
            start = r * rows + off + k
            acc = acc + ext_scr[cb, start:start + rows, :] * cw_ref[k:k + 1, cols]
        conv_scr[r * rows:(r + 1) * rows, cols] = acc
        done.append(acc)
    for work in side_work:
        work(_dep_zero_row(done, zmask_ref[...]))

    cv = conv_scr[...]
    mu = jnp.mean(cv, axis=-1, keepdims=True)
    var = jnp.mean(jnp.square(cv - mu), axis=-1, keepdims=True)
    y = (cv - mu) * lax.rsqrt(var + EPS) * lng_ref[...] + lnb_ref[...]
    conv = _dot((y * jax.nn.sigmoid(y)).astype(BF16), wco_ref[...])

    gates = jnp.concatenate(gate_parts, axis=-1)
    attn = jnp.concatenate(attn_parts, axis=-1)
    merged = gates[:, :D_MODEL] * attn + gates[:, D_MODEL:] * conv
    h_ref[0] = x_ref[0] + _dot(merged.astype(BF16), wout_ref[...])


def _mix(x, o, c, c_meta, gmix, wg, bg, wo, cw, cb, lng, lnb, wco, wout, tm, rows):
    B, L, _ = x.shape
    hb = tm // HALO
    n_halo = L // HALO
    row = lambda b, i: (b, i, 0)
    kern = functools.partial(_mix_kernel, tm=tm, rows=rows)
    return pl.pallas_call(
        kern,
        grid=(B, L // tm),
        in_specs=[
            pl.BlockSpec((1, tm, D_MODEL), row),
            pl.BlockSpec((1, tm, D_MODEL), row),
            pl.BlockSpec((1, tm, D_CONV), row),
            pl.BlockSpec((1, HALO, D_CONV), lambda b, i: (b, jnp.maximum(i * hb - 1, 0), 0)),
            pl.BlockSpec((1, HALO, D_CONV), lambda b, i: (b, jnp.minimum((i + 1) * hb, n_halo - 1), 0)),
            pl.BlockSpec((1, HALO, D_CONV), lambda b, i: (0, 0, 0)),
            _resident((1, D_MODEL)),
            _resident(wg.shape),
            _resident((1, N_GATE)),
            _resident(wo.shape),
            _resident(cw.shape),
            _resident((1, D_CONV)),
            _resident((1, D_CONV)),
            _resident((1, D_CONV)),
            _resident(wco.shape),
            _resident(wout.shape),
            _resident((8, LANES)),
        ],
        out_specs=pl.BlockSpec((1, tm, D_MODEL), row),
        out_shape=jax.ShapeDtypeStruct((B, L, D_MODEL), F32),
        scratch_shapes=[
            pltpu.VMEM((N_CBLK, tm + 2 * HALO, LANES), F32),
            pltpu.VMEM((tm, D_CONV), F32),
        ],
        compiler_params=pltpu.CompilerParams(
            dimension_semantics=("arbitrary", "arbitrary"), vmem_limit_bytes=VMEM_LIMIT),
        name="mix",
    )(x, o, c, c, c, c_meta, gmix, wg, bg, wo, cw, cb, lng, lnb, wco, wout, jnp.zeros((8, LANES), jnp.int32))


def _mlp_kernel(h_ref, gmlp_ref, w1_ref, w2_ref, gfin_ref, y_ref):
    h = h_ref[0]
    u = _rms_f32(h, gmlp_ref[...]).astype(BF16)
    hid = jnp.square(jnp.maximum(_dot(u, w1_ref[...]), 0.0)).astype(BF16)
    h2 = h + _dot(hid, w2_ref[...])
    y_ref[0] = _rms_f32(h2, gfin_ref[...])


def _mlp(h, gmlp, w1, w2, gfin, tm):
    B, L, _ = h.shape
    row = lambda b, i: (b, i, 0)
    return pl.pallas_call(
        _mlp_kernel,
        grid=(B, L // tm),
        in_specs=[
            pl.BlockSpec((1, tm, D_MODEL), row),
            _resident((1, D_MODEL)),
            _resident(w1.shape),
            _resident(w2.shape),
            _resident((1, D_MODEL)),
        ],
        out_specs=pl.BlockSpec((1, tm, D_MODEL), row),
        out_shape=jax.ShapeDtypeStruct((B, L, D_MODEL), F32),
        compiler_params=pltpu.CompilerParams(
            dimension_semantics=("arbitrary", "arbitrary"), vmem_limit_bytes=VMEM_LIMIT),
        name="mlp",
    )(h, gmlp, w1, w2, gfin)


def _rotate_half_cols(w):
    half = ROPE_DIM // 2
    return jnp.concatenate([-w[..., half:], w[..., :half]], axis=-1)


def _rope_table(n):
    inv = ROPE_THETA ** (-jnp.arange(0, ROPE_DIM, 2, dtype=F32) / ROPE_DIM)
    ang = jnp.arange(n, dtype=F32)[:, None] * inv[None, :]
    cos, sin = jnp.cos(ang), jnp.sin(ang)
    return jnp.concatenate([cos, cos, sin, sin], axis=-1)


def kernel(x_prompt, x_sample, meta_tokens, norm_mix, w_in, b_gate, norm_q, w_uq, norm_kv, w_ukv,
           w_o_attn, conv_w, conv_b, conv_ln_g, conv_ln_b, w_conv_out, w_out, norm_mlp, w_mlp_in,
           w_mlp_out, norm_final):
    assert w_in.shape[0] == 1, "single-layer block"
    row2 = lambda a: a.reshape(1, -1).astype(F32)

    w_in0 = w_in[0]
    o_kr = Q_LORA + KV_LORA
    o_conv = o_kr + ROPE_DIM
    o_gate = o_conv + 2 * D_CONV
    w_kr = w_in0[:, o_kr:o_conv]
    wa = jnp.concatenate(
        [w_in0[:, :o_kr], w_kr, _rotate_half_cols(w_kr), w_in0[:, o_conv:o_gate]], axis=1).astype(BF16)
    wg = w_in0[:, o_gate:].astype(BF16)
    wq3 = w_uq[0].reshape(Q_LORA, N_HEADS, NOPE_DIM + ROPE_DIM)
    wq_r = wq3[..., NOPE_DIM:]
    wq = jnp.concatenate([wq3[..., :NOPE_DIM], wq_r, _rotate_half_cols(wq_r)], axis=-1)
    wq = wq.reshape(Q_LORA, N_HEADS * HEAD_W).astype(BF16)
    wkv = w_ukv[0].astype(BF16)
    wo = w_o_attn[0].astype(BF16)
    wco = w_conv_out[0].astype(BF16)
    wout = w_out[0].astype(BF16)
    w1 = w_mlp_in[0].astype(BF16)
    w2 = w_mlp_out[0].astype(BF16)
    cw = jnp.pad(conv_w[0].astype(F32), ((0, 1), (0, 0)))
    gmix, gq, gkv = row2(norm_mix[0]), row2(norm_q[0]), row2(norm_kv[0])
    bg, cb = row2(b_gate[0]), row2(conv_b[0])
    lng, lnb = row2(conv_ln_g[0]), row2(conv_ln_b[0])
    gmlp, gfin = row2(norm_mlp[0]), row2(norm_final)

    max_len = max(x_prompt.shape[1], x_sample.shape[1], META_ROWS)
    tab = _rope_table(N_META + max_len)

    meta = jnp.pad(meta_tokens.astype(F32), ((0, META_ROWS - N_META), (0, 0)))[None]
    _, k_meta, v_meta, c_meta = _project(meta, tab[:META_ROWS], gmix, wa, gq, wq, gkv, wkv, META_ROWS)

    def encode(x):
        L = x.shape[1]
        q, k, v, c = _project(x, tab[N_META:N_META + L], gmix, wa, gq, wq, gkv, wkv, 512)
        o = _attention(q, k, v, k_meta, v_meta, 2048, 512)
        h = _mix(x, o, c, c_meta, gmix, wg, bg, wo, cw, cb, lng, lnb, wco, wout, 256, 64)
        return _mlp(h, gmlp, w1, w2, gfin, 512)

    return encode(x_prompt), encode(x_sample)
```

```python
import functools
import math

import jax
import jax.numpy as jnp
from jax import lax
from jax.experimental import pallas as pl
from jax.experimental.pallas import tpu as pltpu

D_MODEL = 1024
N_META = 16
N_HEADS = 8
NOPE_DIM = 128
ROPE_DIM = 64
V_DIM = 128
Q_LORA = 512
KV_LORA = 256
ROPE_THETA = 10000.0
D_CONV = D_MODEL
CONV_WIDTH = 31
CONV_PAD = CONV_WIDTH // 2
D_FF = 4 * D_MODEL
N_GATE = 2 * D_MODEL
EPS = 1e-6

LANES = 128
HEAD_W = 2 * LANES
META_ROWS = LANES
HALO = 16
LAT_W = Q_LORA + KV_LORA + LANES
N_CBLK = D_CONV // LANES
Q_SCALE = math.log2(math.e) / math.sqrt(NOPE_DIM + ROPE_DIM)
NEG_BIG = -1e30
VMEM_LIMIT = 56 * 1024 * 1024
ATTN_SUB = 256

F32 = jnp.float32
BF16 = jnp.bfloat16


def _rms_f32(x, g):
    return x * lax.rsqrt(jnp.mean(x * x, axis=-1, keepdims=True) + EPS) * g


def _dot(a, b):
    return jnp.dot(a, b, preferred_element_type=F32)


def _dot_nt(a, b):
    return lax.dot_general(a, b, (((1,), (1,)), ((), ())), preferred_element_type=F32)


def _rope_pair_sum(t):
    return t + pltpu.roll(t, ROPE_DIM, 1)


def _resident(shape):
    return pl.BlockSpec(shape, lambda *_: (0,) * len(shape), pipeline_mode=pl.Buffered(1))


def _proj_kernel(x_ref, tab_ref, gmix_ref, wa_ref, gq_ref, wq_ref, gkv_ref, wkv_ref,
                 q_ref, k_ref, vt_ref, c_ref):
    x = x_ref[0]
    u = _rms_f32(x, gmix_ref[...]).astype(BF16)
    tab = tab_ref[...]
    lat = _dot(u, wa_ref[:, :LAT_W])

    rq = _rms_f32(lat[:, :Q_LORA], gq_ref[...]).astype(BF16)
    qa = _dot(rq, wq_ref[...])
    for h in range(N_HEADS):
        base = h * HEAD_W
        qn = qa[:, base:base + LANES] * Q_SCALE
        qr = _rope_pair_sum(qa[:, base + LANES:base + HEAD_W] * tab) * Q_SCALE
        q_ref[0, h, :, :LANES] = qn.astype(BF16)
        q_ref[0, h, :, LANES:] = qr.astype(BF16)

    kr = _rope_pair_sum(lat[:, Q_LORA + KV_LORA:LAT_W] * tab)
    lane = lax.broadcasted_iota(jnp.int32, kr.shape, 1)
    kr = jnp.where(lane < ROPE_DIM, kr, 0.0).astype(BF16)

    rkv = _rms_f32(lat[:, Q_LORA:Q_LORA + KV_LORA], gkv_ref[...]).astype(BF16)
    kv = _dot(rkv, wkv_ref[...])
    for h in range(N_HEADS):
        base = h * HEAD_W
        k_ref[0, h, :, :LANES] = kv[:, base:base + LANES].astype(BF16)
        k_ref[0, h, :, LANES:] = kr
        vt_ref[0, h] = kv[:, base + LANES:base + HEAD_W].T.astype(BF16)

    ab = _dot(u, wa_ref[:, LAT_W:])
    c_ref[0] = ab[:, :D_CONV] * jax.nn.sigmoid(ab[:, D_CONV:])


def _project(x, tab, gmix, wa, gq, wq, gkv, wkv, tm):
    B, L, _ = x.shape
    return pl.pallas_call(
        _proj_kernel,
        grid=(B, L // tm),
        in_specs=[
            pl.BlockSpec((1, tm, D_MODEL), lambda b, i: (b, i, 0)),
            pl.BlockSpec((tm, LANES), lambda b, i: (i, 0)),
            _resident((1, D_MODEL)),
            _resident(wa.shape),
            _resident((1, Q_LORA)),
            _resident(wq.shape),
            _resident((1, KV_LORA)),
            _resident(wkv.shape),
        ],
        out_specs=[
            pl.BlockSpec((1, N_HEADS, tm, HEAD_W), lambda b, i: (b, 0, i, 0)),
            pl.BlockSpec((1, N_HEADS, tm, HEAD_W), lambda b, i: (b, 0, i, 0)),
            pl.BlockSpec((1, N_HEADS, V_DIM, tm), lambda b, i: (b, 0, 0, i)),
            pl.BlockSpec((1, tm, D_CONV), lambda b, i: (b, i, 0)),
        ],
        out_shape=[
            jax.ShapeDtypeStruct((B, N_HEADS, L, HEAD_W), BF16),
            jax.ShapeDtypeStruct((B, N_HEADS, L, HEAD_W), BF16),
            jax.ShapeDtypeStruct((B, N_HEADS, V_DIM, L), BF16),
            jax.ShapeDtypeStruct((B, L, D_CONV), F32),
        ],
        compiler_params=pltpu.CompilerParams(
            dimension_semantics=("arbitrary", "arbitrary"), vmem_limit_bytes=VMEM_LIMIT),
        name="proj",
    )(x, tab, gmix, wa, gq, wq, gkv, wkv)


def _tree(op, xs):
    xs = list(xs)
    while len(xs) > 1:
        xs = [op(xs[i], xs[i + 1]) if i + 1 < len(xs) else xs[i] for i in range(0, len(xs), 2)]
    return xs[0]


def _row_groups(a):
    return [a[r:r + 8] for r in range(0, a.shape[0], 8)]


def _attn_kernel(q_ref, k_ref, vt_ref, km_ref, vmt_ref, o_ref, s_scr, p_scr, *, seq, tk, n_sub):
    n_chunks = seq // tk
    n_groups = (seq + META_ROWS) // LANES
    groups_per_chunk = -(-n_groups // (n_chunks + 1))
    key_row = lax.broadcasted_iota(jnp.int32, (META_ROWS, ATTN_SUB), 0)

    def q_rows(t):
        return q_ref[0, 0, t * ATTN_SUB:(t + 1) * ATTN_SUB, :]

    def score_meta(t):
        sm = jnp.where(key_row < N_META, _dot_nt(km_ref[0, 0], q_rows(t)), NEG_BIG)
        s_scr[t % 2, seq:, :] = sm
        return _tree(jnp.maximum, _row_groups(sm))

    def score_chunk(t, j, m_acc):
        s = _dot_nt(k_ref[0, 0, j * tk:(j + 1) * tk, :], q_rows(t))
        s_scr[t % 2, j * tk:(j + 1) * tk, :] = s
        return jnp.maximum(m_acc, _tree(jnp.maximum, _row_groups(s)))

    def prob_group(t, g, m, l_acc):
        rows = slice(g * LANES, (g + 1) * LANES)
        p = jnp.exp2(s_scr[t % 2, rows, :] - m)
        p_scr[t % 2, rows, :] = p.astype(BF16)
        return l_acc + _tree(jnp.add, _row_groups(p))

    def weighted_values(t, l_acc):
        l = jnp.sum(l_acc, axis=0, keepdims=True)
        acc = _dot(vt_ref[0, 0], p_scr[t % 2, :seq, :]) + _dot(vmt_ref[0, 0], p_scr[t % 2, seq:, :])
        o_ref[0, t * ATTN_SUB:(t + 1) * ATTN_SUB, :] = (acc / l).T.astype(BF16)

    row_max = {}
    row_sum = {}
    for stage in range(n_sub + 2):
        t_score, t_prob, t_out = stage, stage - 1, stage - 2
        if 0 <= t_out < n_sub:
            weighted_values(t_out, row_sum.pop(t_out))
        do_score = t_score < n_sub
        do_prob = 0 <= t_prob < n_sub
        m_acc = score_meta(t_score) if do_score else None
        l_acc = jnp.zeros((8, ATTN_SUB), F32) if do_prob else None
        g = 0
        for j in range(n_chunks + 1):
            if do_score and j < n_chunks:
                m_acc = score_chunk(t_score, j, m_acc)
            if do_prob:
                for _ in range(groups_per_chunk):
                    if g < n_groups:
                        l_acc = prob_group(t_prob, g, row_max[t_prob], l_acc)
                        g += 1
        if do_score:
            row_max[t_score] = jnp.max(m_acc, axis=0, keepdims=True)
        if do_prob:
            row_sum[t_prob] = l_acc


def _attention(q, k, vt, k_meta, vt_meta, tq, tk):
    B, H, L, _ = q.shape
    kern = functools.partial(_attn_kernel, seq=L, tk=tk, n_sub=tq // ATTN_SUB)
    return pl.pallas_call(
        kern,
        grid=(B, H, L // tq),
        in_specs=[
            pl.BlockSpec((1, 1, tq, HEAD_W), lambda b, h, i: (b, h, i, 0)),
            pl.BlockSpec((1, 1, L, HEAD_W), lambda b, h, i: (b, h, 0, 0)),
            pl.BlockSpec((1, 1, V_DIM, L), lambda b, h, i: (b, h, 0, 0)),
            pl.BlockSpec((1, 1, META_ROWS, HEAD_W), lambda b, h, i: (0, h, 0, 0)),
            pl.BlockSpec((1, 1, V_DIM, META_ROWS), lambda b, h, i: (0, h, 0, 0)),
        ],
        out_specs=pl.BlockSpec((1, tq, V_DIM), lambda b, h, i: (b, i, h)),
        out_shape=jax.ShapeDtypeStruct((B, L, H * V_DIM), BF16),
        scratch_shapes=[
            pltpu.VMEM((2, L + META_ROWS, ATTN_SUB), F32),
            pltpu.VMEM((2, L + META_ROWS, ATTN_SUB), BF16),
        ],
        compiler_params=pltpu.CompilerParams(
            dimension_semantics=("arbitrary", "arbitrary", "arbitrary"), vmem_limit_bytes=VMEM_LIMIT),
        name="attn",
    )(q, k, vt, k_meta, vt_meta)


def _mix_kernel(x_ref, o_ref, c_ref, cprev_ref, cnext_ref, cmeta_ref,
                gmix_ref, wg_ref, bg_ref, wo_ref, cw_ref, cb_ref, lng_ref, lnb_ref, wco_ref, wout_ref,
                h_ref, ext_scr, conv_scr, *, tm, rows):
    i = pl.program_id(1)
    last = pl.num_programs(1) - 1

    prev = jnp.where(i == 0, cmeta_ref[0], cprev_ref[0])
    nxt = jnp.where(i == last, 0.0, cnext_ref[0])
    for cb in range(N_CBLK):
        cols = slice(cb * LANES, (cb + 1) * LANES)
        ext_scr[cb, :HALO, :] = prev[:, cols]
        ext_scr[cb, HALO:HALO + tm, :] = c_ref[0, :, cols]
        ext_scr[cb, HALO + tm:, :] = nxt[:, cols]

    off = HALO - CONV_PAD
    for r in range(tm // rows):
        for cb in range(N_CBLK):
            cols = slice(cb * LANES, (cb + 1) * LANES)
            acc = jnp.broadcast_to(cb_ref[:, cols], (rows, LANES))
            for k in range(CONV_WIDTH):
                start = r * rows + off + k
                acc = acc + ext_scr[cb, start:start + rows, :] * cw_ref[k:k + 1, cols]
            conv_scr[r * rows:(r + 1) * rows, cols] = acc

    cv = conv_scr[...]
    mu = jnp.mean(cv, axis=-1, keepdims=True)
    var = jnp.mean(jnp.square(cv - mu), axis=-1, keepdims=True)
    y = (cv - mu) * lax.rsqrt(var + EPS) * lng_ref[...] + lnb_ref[...]
    conv = _dot((y * jax.nn.sigmoid(y)).astype(BF16), wco_ref[...])

    x = x_ref[0]
    u = _rms_f32(x, gmix_ref[...]).astype(BF16)
    gates = jax.nn.sigmoid(_dot(u, wg_ref[...]) + bg_ref[...])
    attn = _dot(o_ref[0], wo_ref[...])
    merged = gates[:, :D_MODEL] * attn + gates[:, D_MODEL:] * conv
    h_ref[0] = x + _dot(merged.astype(BF16), wout_ref[...])


def _mix(x, o, c, c_meta, gmix, wg, bg, wo, cw, cb, lng, lnb, wco, wout, tm, rows):
    B, L, _ = x.shape
    hb = tm // HALO
    n_halo = L // HALO
    row = lambda b, i: (b, i, 0)
    kern = functools.partial(_mix_kernel, tm=tm, rows=rows)
    return pl.pallas_call(
        kern,
        grid=(B, L // tm),
        in_specs=[
            pl.BlockSpec((1, tm, D_MODEL), row),
            pl.BlockSpec((1, tm, D_MODEL), row),
            pl.BlockSpec((1, tm, D_CONV), row),
            pl.BlockSpec((1, HALO, D_CONV), lambda b, i: (b, jnp.maximum(i * hb - 1, 0), 0)),
            pl.BlockSpec((1, HALO, D_CONV), lambda b, i: (b, jnp.minimum((i + 1) * hb, n_halo - 1), 0)),
            pl.BlockSpec((1, HALO, D_CONV), lambda b, i: (0, 0, 0)),
            _resident((1, D_MODEL)),
            _resident(wg.shape),
            _resident((1, N_GATE)),
            _resident(wo.shape),
            _resident(cw.shape),
            _resident((1, D_CONV)),
            _resident((1, D_CONV)),
            _resident((1, D_CONV)),
            _resident(wco.shape),
            _resident(wout.shape),
        ],
        out_specs=pl.BlockSpec((1, tm, D_MODEL), row),
        out_shape=jax.ShapeDtypeStruct((B, L, D_MODEL), F32),
        scratch_shapes=[
            pltpu.VMEM((N_CBLK, tm + 2 * HALO, LANES), F32),
            pltpu.VMEM((tm, D_CONV), F32),
        ],
        compiler_params=pltpu.CompilerParams(
            dimension_semantics=("arbitrary", "arbitrary"), vmem_limit_bytes=VMEM_LIMIT),
        name="mix",
    )(x, o, c, c, c, c_meta, gmix, wg, bg, wo, cw, cb, lng, lnb, wco, wout)


def _mlp_kernel(h_ref, gmlp_ref, w1_ref, w2_ref, gfin_ref, y_ref):
    h = h_ref[0]
    u = _rms_f32(h, gmlp_ref[...]).astype(BF16)
    hid = jnp.square(jnp.maximum(_dot(u, w1_ref[...]), 0.0)).astype(BF16)
    h2 = h + _dot(hid, w2_ref[...])
    y_ref[0] = _rms_f32(h2, gfin_ref[...])


def _mlp(h, gmlp, w1, w2, gfin, tm):
    B, L, _ = h.shape
    row = lambda b, i: (b, i, 0)
    return pl.pallas_call(
        _mlp_kernel,
        grid=(B, L // tm),
        in_specs=[
            pl.BlockSpec((1, tm, D_MODEL), row),
            _resident((1, D_MODEL)),
            _resident(w1.shape),
            _resident(w2.shape),
            _resident((1, D_MODEL)),
        ],
        out_specs=pl.BlockSpec((1, tm, D_MODEL), row),
        out_shape=jax.ShapeDtypeStruct((B, L, D_MODEL), F32),
        compiler_params=pltpu.CompilerParams(
            dimension_semantics=("arbitrary", "arbitrary"), vmem_limit_bytes=VMEM_LIMIT),
        name="mlp",
    )(h, gmlp, w1, w2, gfin)


def _rotate_half_cols(w):
    half = ROPE_DIM // 2
    return jnp.concatenate([-w[..., half:], w[..., :half]], axis=-1)


def _rope_table(n):
    inv = ROPE_THETA ** (-jnp.arange(0, ROPE_DIM, 2, dtype=F32) / ROPE_DIM)
    ang = jnp.arange(n, dtype=F32)[:, None] * inv[None, :]
    cos, sin = jnp.cos(ang), jnp.sin(ang)
    return jnp.concatenate([cos, cos, sin, sin], axis=-1)


def kernel(x_prompt, x_sample, meta_tokens, norm_mix, w_in, b_gate, norm_q, w_uq, norm_kv, w_ukv,
           w_o_attn, conv_w, conv_b, conv_ln_g, conv_ln_b, w_conv_out, w_out, norm_mlp, w_mlp_in,
           w_mlp_out, norm_final):
    assert w_in.shape[0] == 1, "single-layer block"
    row2 = lambda a: a.reshape(1, -1).astype(F32)

    w_in0 = w_in[0]
    o_kr = Q_LORA + KV_LORA
    o_conv = o_kr + ROPE_DIM
    o_gate = o_conv + 2 * D_CONV
    w_kr = w_in0[:, o_kr:o_conv]
    wa = jnp.concatenate(
        [w_in0[:, :o_kr], w_kr, _rotate_half_cols(w_kr), w_in0[:, o_conv:o_gate]], axis=1).astype(BF16)
    wg = w_in0[:, o_gate:].astype(BF16)
    wq3 = w_uq[0].reshape(Q_LORA, N_HEADS, NOPE_DIM + ROPE_DIM)
    wq_r = wq3[..., NOPE_DIM:]
    wq = jnp.concatenate([wq3[..., :NOPE_DIM], wq_r, _rotate_half_cols(wq_r)], axis=-1)
    wq = wq.reshape(Q_LORA, N_HEADS * HEAD_W).astype(BF16)
    wkv = w_ukv[0].astype(BF16)
    wo = w_o_attn[0].astype(BF16)
    wco = w_conv_out[0].astype(BF16)
    wout = w_out[0].astype(BF16)
    w1 = w_mlp_in[0].astype(BF16)
    w2 = w_mlp_out[0].astype(BF16)
    cw = jnp.pad(conv_w[0].astype(F32), ((0, 1), (0, 0)))
    gmix, gq, gkv = row2(norm_mix[0]), row2(norm_q[0]), row2(norm_kv[0])
    bg, cb = row2(b_gate[0]), row2(conv_b[0])
    lng, lnb = row2(conv_ln_g[0]), row2(conv_ln_b[0])
    gmlp, gfin = row2(norm_mlp[0]), row2(norm_final)

    max_len = max(x_prompt.shape[1], x_sample.shape[1], META_ROWS)
    tab = _rope_table(N_META + max_len)

    meta = jnp.pad(meta_tokens.astype(F32), ((0, META_ROWS - N_META), (0, 0)))[None]
    _, k_meta, vt_meta, c_meta = _project(meta, tab[:META_ROWS], gmix, wa, gq, wq, gkv, wkv, META_ROWS)

    def encode(x):
        L = x.shape[1]
        q, k, vt, c = _project(x, tab[N_META:N_META + L], gmix, wa, gq, wq, gkv, wkv, 512)
        o = _attention(q, k, vt, k_meta, vt_meta, 2048, 512)
        h = _mix(x, o, c, c_meta, gmix, wg, bg, wo, cw, cb, lng, lnb, wco, wout, 256, 64)
        return _mlp(h, gmlp, w1, w2, gfin, 512)

    return encode(x_prompt), encode(x_sample)
```

```python
import functools
import math

import jax
import jax.numpy as jnp
from jax import lax
from jax.experimental import pallas as pl
from jax.experimental.pallas import tpu as pltpu

D_MODEL = 1024
N_META = 16
N_HEADS = 8
NOPE_DIM = 128
ROPE_DIM = 64
V_DIM = 128
Q_LORA = 512
KV_LORA = 256
ROPE_THETA = 10000.0
D_CONV = D_MODEL
CONV_WIDTH = 31
CONV_PAD = CONV_WIDTH // 2
D_FF = 4 * D_MODEL
N_GATE = 2 * D_MODEL
EPS = 1e-6

LANES = 128
HEAD_W = 2 * LANES
META_ROWS = LANES
HALO = 16
LAT_W = Q_LORA + KV_LORA + LANES
N_CBLK = D_CONV // LANES
Q_SCALE = math.log2(math.e) / math.sqrt(NOPE_DIM + ROPE_DIM)
NEG_BIG = -1e30
VMEM_LIMIT = 56 * 1024 * 1024
ATTN_SUB = 256

F32 = jnp.float32
BF16 = jnp.bfloat16


def _rms_f32(x, g):
    return x * lax.rsqrt(jnp.mean(x * x, axis=-1, keepdims=True) + EPS) * g


def _dot(a, b):
    return jnp.dot(a, b, preferred_element_type=F32)


def _dot_nt(a, b):
    return lax.dot_general(a, b, (((1,), (1,)), ((), ())), preferred_element_type=F32)


def _rope_pair_sum(t):
    return t + pltpu.roll(t, ROPE_DIM, 1)


def _resident(shape):
    return pl.BlockSpec(shape, lambda *_: (0,) * len(shape), pipeline_mode=pl.Buffered(1))


def _proj_kernel(x_ref, tab_ref, gmix_ref, wa_ref, gq_ref, wq_ref, gkv_ref, wkv_ref,
                 q_ref, k_ref, vt_ref, c_ref):
    x = x_ref[0]
    u = _rms_f32(x, gmix_ref[...]).astype(BF16)
    tab = tab_ref[...]
    lat = _dot(u, wa_ref[:, :LAT_W])

    rq = _rms_f32(lat[:, :Q_LORA], gq_ref[...]).astype(BF16)
    qa = _dot(rq, wq_ref[...])
    for h in range(N_HEADS):
        base = h * HEAD_W
        qn = qa[:, base:base + LANES] * Q_SCALE
        qr = _rope_pair_sum(qa[:, base + LANES:base + HEAD_W] * tab) * Q_SCALE
        q_ref[0, h, :, :LANES] = qn.astype(BF16)
        q_ref[0, h, :, LANES:] = qr.astype(BF16)

    kr = _rope_pair_sum(lat[:, Q_LORA + KV_LORA:LAT_W] * tab)
    lane = lax.broadcasted_iota(jnp.int32, kr.shape, 1)
    kr = jnp.where(lane < ROPE_DIM, kr, 0.0).astype(BF16)

    rkv = _rms_f32(lat[:, Q_LORA:Q_LORA + KV_LORA], gkv_ref[...]).astype(BF16)
    kv = _dot(rkv, wkv_ref[...])
    for h in range(N_HEADS):
        base = h * HEAD_W
        k_ref[0, h, :, :LANES] = kv[:, base:base + LANES].astype(BF16)
        k_ref[0, h, :, LANES:] = kr
        vt_ref[0, h] = kv[:, base + LANES:base + HEAD_W].T.astype(BF16)

    ab = _dot(u, wa_ref[:, LAT_W:])
    c_ref[0] = ab[:, :D_CONV] * jax.nn.sigmoid(ab[:, D_CONV:])


def _project(x, tab, gmix, wa, gq, wq, gkv, wkv, tm):
    B, L, _ = x.shape
    return pl.pallas_call(
        _proj_kernel,
        grid=(B, L // tm),
        in_specs=[
            pl.BlockSpec((1, tm, D_MODEL), lambda b, i: (b, i, 0)),
            pl.BlockSpec((tm, LANES), lambda b, i: (i, 0)),
            _resident((1, D_MODEL)),
            _resident(wa.shape),
            _resident((1, Q_LORA)),
            _resident(wq.shape),
            _resident((1, KV_LORA)),
            _resident(wkv.shape),
        ],
        out_specs=[
            pl.BlockSpec((1, N_HEADS, tm, HEAD_W), lambda b, i: (b, 0, i, 0)),
            pl.BlockSpec((1, N_HEADS, tm, HEAD_W), lambda b, i: (b, 0, i, 0)),
            pl.BlockSpec((1, N_HEADS, V_DIM, tm), lambda b, i: (b, 0, 0, i)),
            pl.BlockSpec((1, tm, D_CONV), lambda b, i: (b, i, 0)),
        ],
        out_shape=[
            jax.ShapeDtypeStruct((B, N_HEADS, L, HEAD_W), BF16),
            jax.ShapeDtypeStruct((B, N_HEADS, L, HEAD_W), BF16),
            jax.ShapeDtypeStruct((B, N_HEADS, V_DIM, L), BF16),
            jax.ShapeDtypeStruct((B, L, D_CONV), F32),
        ],
        compiler_params=pltpu.CompilerParams(
            dimension_semantics=("arbitrary", "arbitrary"), vmem_limit_bytes=VMEM_LIMIT),
        name="proj",
    )(x, tab, gmix, wa, gq, wq, gkv, wkv)


def _tree(op, xs):
    xs = list(xs)
    while len(xs) > 1:
        xs = [op(xs[i], xs[i + 1]) if i + 1 < len(xs) else xs[i] for i in range(0, len(xs), 2)]
    return xs[0]


def _row_groups(a):
    return [a[r:r + 8] for r in range(0, a.shape[0], 8)]


def _attn_kernel(q_ref, k_ref, vt_ref, km_ref, vmt_ref, o_ref, s_scr, p_scr, *, seq, tk, n_sub):
    n_chunks = seq // tk
    n_groups = (seq + META_ROWS) // LANES
    groups_per_chunk = -(-n_groups // (n_chunks + 1))
    key_row = lax.broadcasted_iota(jnp.int32, (META_ROWS, ATTN_SUB), 0)

    def q_rows(t):
        return q_ref[0, 0, t * ATTN_SUB:(t + 1) * ATTN_SUB, :]

    def score_meta(t):
        sm = jnp.where(key_row < N_META, _dot_nt(km_ref[0, 0], q_rows(t)), NEG_BIG)
        s_scr[t % 2, seq:, :] = sm
        return _tree(jnp.maximum, _row_groups(sm))

    def score_chunk(t, j, m_acc):
        s = _dot_nt(k_ref[0, 0, j * tk:(j + 1) * tk, :], q_rows(t))
        s_scr[t % 2, j * tk:(j + 1) * tk, :] = s
        return jnp.maximum(m_acc, _tree(jnp.maximum, _row_groups(s)))

    def prob_group(t, g, m, l_acc):
        rows = slice(g * LANES, (g + 1) * LANES)
        p = jnp.exp2(s_scr[t % 2, rows, :] - m)
        p_scr[t % 2, rows, :] = p.astype(BF16)
        return l_acc + _tree(jnp.add, _row_groups(p))

    def weighted_values(t, l_acc):
        l = jnp.sum(l_acc, axis=0, keepdims=True)
        acc = _dot(vt_ref[0, 0], p_scr[t % 2, :seq, :]) + _dot(vmt_ref[0, 0], p_scr[t % 2, seq:, :])
        o_ref[0, 0, t * ATTN_SUB:(t + 1) * ATTN_SUB, :] = (acc / l).T.astype(BF16)

    row_max = {}
    row_sum = {}
    for stage in range(n_sub + 2):
        t_score, t_prob, t_out = stage, stage - 1, stage - 2
        if 0 <= t_out < n_sub:
            weighted_values(t_out, row_sum.pop(t_out))
        do_score = t_score < n_sub
        do_prob = 0 <= t_prob < n_sub
        m_acc = score_meta(t_score) if do_score else None
        l_acc = jnp.zeros((8, ATTN_SUB), F32) if do_prob else None
        g = 0
        for j in range(n_chunks + 1):
            if do_score and j < n_chunks:
                m_acc = score_chunk(t_score, j, m_acc)
            if do_prob:
                for _ in range(groups_per_chunk):
                    if g < n_groups:
                        l_acc = prob_group(t_prob, g, row_max[t_prob], l_acc)
                        g += 1
        if do_score:
            row_max[t_score] = jnp.max(m_acc, axis=0, keepdims=True)
        if do_prob:
            row_sum[t_prob] = l_acc


def _attention(q, k, vt, k_meta, vt_meta, tq, tk):
    B, H, L, _ = q.shape
    kern = functools.partial(_attn_kernel, seq=L, tk=tk, n_sub=tq // ATTN_SUB)
    return pl.pallas_call(
        kern,
        grid=(B, H, L // tq),
        in_specs=[
            pl.BlockSpec((1, 1, tq, HEAD_W), lambda b, h, i: (b, h, i, 0)),
            pl.BlockSpec((1, 1, L, HEAD_W), lambda b, h, i: (b, h, 0, 0)),
            pl.BlockSpec((1, 1, V_DIM, L), lambda b, h, i: (b, h, 0, 0)),
            pl.BlockSpec((1, 1, META_ROWS, HEAD_W), lambda b, h, i: (0, h, 0, 0)),
            pl.BlockSpec((1, 1, V_DIM, META_ROWS), lambda b, h, i: (0, h, 0, 0)),
        ],
        out_specs=pl.BlockSpec((1, 1, tq, V_DIM), lambda b, h, i: (b, h, i, 0)),
        out_shape=jax.ShapeDtypeStruct((B, H, L, V_DIM), BF16),
        scratch_shapes=[
            pltpu.VMEM((2, L + META_ROWS, ATTN_SUB), F32),
            pltpu.VMEM((2, L + META_ROWS, ATTN_SUB), BF16),
        ],
        compiler_params=pltpu.CompilerParams(
            dimension_semantics=("arbitrary", "arbitrary", "arbitrary"), vmem_limit_bytes=VMEM_LIMIT),
        name="attn",
    )(q, k, vt, k_meta, vt_meta)


def _mix_kernel(x_ref, o_ref, c_ref, cprev_ref, cnext_ref, cmeta_ref,
                gmix_ref, wg_ref, bg_ref, wo_ref, cw_ref, cb_ref, lng_ref, lnb_ref, wco_ref, wout_ref,
                h_ref, ext_scr, conv_scr, *, tm, rows):
    i = pl.program_id(1)
    last = pl.num_programs(1) - 1

    prev = jnp.where(i == 0, cmeta_ref[0], cprev_ref[0])
    nxt = jnp.where(i == last, 0.0, cnext_ref[0])
    for cb in range(N_CBLK):
        cols = slice(cb * LANES, (cb + 1) * LANES)
        ext_scr[cb, :HALO, :] = prev[:, cols]
        ext_scr[cb, HALO:HALO + tm, :] = c_ref[0, :, cols]
        ext_scr[cb, HALO + tm:, :] = nxt[:, cols]

    off = HALO - CONV_PAD
    for r in range(tm // rows):
        for cb in range(N_CBLK):
            cols = slice(cb * LANES, (cb + 1) * LANES)
            acc = jnp.broadcast_to(cb_ref[:, cols], (rows, LANES))
            for k in range(CONV_WIDTH):
                start = r * rows + off + k
                acc = acc + ext_scr[cb, start:start + rows, :] * cw_ref[k:k + 1, cols]
            conv_scr[r * rows:(r + 1) * rows, cols] = acc

    cv = conv_scr[...]
    mu = jnp.mean(cv, axis=-1, keepdims=True)
    var = jnp.mean(jnp.square(cv - mu), axis=-1, keepdims=True)
    y = (cv - mu) * lax.rsqrt(var + EPS) * lng_ref[...] + lnb_ref[...]
    conv = _dot((y * jax.nn.sigmoid(y)).astype(BF16), wco_ref[...])

    x = x_ref[0]
    u = _rms_f32(x, gmix_ref[...]).astype(BF16)
    gates = jax.nn.sigmoid(_dot(u, wg_ref[...]) + bg_ref[...])
    o = jnp.concatenate([o_ref[0, h] for h in range(N_HEADS)], axis=-1)
    attn = _dot(o, wo_ref[...])
    merged = gates[:, :D_MODEL] * attn + gates[:, D_MODEL:] * conv
    h_ref[0] = x + _dot(merged.astype(BF16), wout_ref[...])


def _mix(x, o, c, c_meta, gmix, wg, bg, wo, cw, cb, lng, lnb, wco, wout, tm, rows):
    B, L, _ = x.shape
    hb = tm // HALO
    n_halo = L // HALO
    row = lambda b, i: (b, i, 0)
    kern = functools.partial(_mix_kernel, tm=tm, rows=rows)
    return pl.pallas_call(
        kern,
        grid=(B, L // tm),
        in_specs=[
            pl.BlockSpec((1, tm, D_MODEL), row),
            pl.BlockSpec((1, N_HEADS, tm, V_DIM), lambda b, i: (b, 0, i, 0)),
            pl.BlockSpec((1, tm, D_CONV), row),
            pl.BlockSpec((1, HALO, D_CONV), lambda b, i: (b, jnp.maximum(i * hb - 1, 0), 0)),
            pl.BlockSpec((1, HALO, D_CONV), lambda b, i: (b, jnp.minimum((i + 1) * hb, n_halo - 1), 0)),
            pl.BlockSpec((1, HALO, D_CONV), lambda b, i: (0, 0, 0)),
            _resident((1, D_MODEL)),
            _resident(wg.shape),
            _resident((1, N_GATE)),
            _resident(wo.shape),
            _resident(cw.shape),
            _resident((1, D_CONV)),
            _resident((1, D_CONV)),
            _resident((1, D_CONV)),
            _resident(wco.shape),
            _resident(wout.shape),
        ],
        out_specs=pl.BlockSpec((1, tm, D_MODEL), row),
        out_shape=jax.ShapeDtypeStruct((B, L, D_MODEL), F32),
        scratch_shapes=[
            pltpu.VMEM((N_CBLK, tm + 2 * HALO, LANES), F32),
            pltpu.VMEM((tm, D_CONV), F32),
        ],
        compiler_params=pltpu.CompilerParams(
            dimension_semantics=("arbitrary", "arbitrary"), vmem_limit_bytes=VMEM_LIMIT),
        name="mix",
    )(x, o, c, c, c, c_meta, gmix, wg, bg, wo, cw, cb, lng, lnb, wco, wout)


def _mlp_kernel(h_ref, gmlp_ref, w1_ref, w2_ref, gfin_ref, y_ref):
    h = h_ref[0]
    u = _rms_f32(h, gmlp_ref[...]).astype(BF16)
    hid = jnp.square(jnp.maximum(_dot(u, w1_ref[...]), 0.0)).astype(BF16)
    h2 = h + _dot(hid, w2_ref[...])
    y_ref[0] = _rms_f32(h2, gfin_ref[...])


def _mlp(h, gmlp, w1, w2, gfin, tm):
    B, L, _ = h.shape
    row = lambda b, i: (b, i, 0)
    return pl.pallas_call(
        _mlp_kernel,
        grid=(B, L // tm),
        in_specs=[
            pl.BlockSpec((1, tm, D_MODEL), row),
            _resident((1, D_MODEL)),
            _resident(w1.shape),
            _resident(w2.shape),
            _resident((1, D_MODEL)),
        ],
        out_specs=pl.BlockSpec((1, tm, D_MODEL), row),
        out_shape=jax.ShapeDtypeStruct((B, L, D_MODEL), F32),
        compiler_params=pltpu.CompilerParams(
            dimension_semantics=("arbitrary", "arbitrary"), vmem_limit_bytes=VMEM_LIMIT),
        name="mlp",
    )(h, gmlp, w1, w2, gfin)


def _rotate_half_cols(w):
    half = ROPE_DIM // 2
    return jnp.concatenate([-w[..., half:], w[..., :half]], axis=-1)


def _rope_table(n):
    inv = ROPE_THETA ** (-jnp.arange(0, ROPE_DIM, 2, dtype=F32) / ROPE_DIM)
    ang = jnp.arange(n, dtype=F32)[:, None] * inv[None, :]
    cos, sin = jnp.cos(ang), jnp.sin(ang)
    return jnp.concatenate([cos, cos, sin, sin], axis=-1)


def kernel(x_prompt, x_sample, meta_tokens, norm_mix, w_in, b_gate, norm_q, w_uq, norm_kv, w_ukv,
           w_o_attn, conv_w, conv_b, conv_ln_g, conv_ln_b, w_conv_out, w_out, norm_mlp, w_mlp_in,
           w_mlp_out, norm_final):
    assert w_in.shape[0] == 1, "single-layer block"
    row2 = lambda a: a.reshape(1, -1).astype(F32)

    w_in0 = w_in[0]
    o_kr = Q_LORA + KV_LORA
    o_conv = o_kr + ROPE_DIM
    o_gate = o_conv + 2 * D_CONV
    w_kr = w_in0[:, o_kr:o_conv]
    wa = jnp.concatenate(
        [w_in0[:, :o_kr], w_kr, _rotate_half_cols(w_kr), w_in0[:, o_conv:o_gate]], axis=1).astype(BF16)
    wg = w_in0[:, o_gate:].astype(BF16)
    wq3 = w_uq[0].reshape(Q_LORA, N_HEADS, NOPE_DIM + ROPE_DIM)
    wq_r = wq3[..., NOPE_DIM:]
    wq = jnp.concatenate([wq3[..., :NOPE_DIM], wq_r, _rotate_half_cols(wq_r)], axis=-1)
    wq = wq.reshape(Q_LORA, N_HEADS * HEAD_W).astype(BF16)
    wkv = w_ukv[0].astype(BF16)
    wo = w_o_attn[0].astype(BF16)
    wco = w_conv_out[0].astype(BF16)
    wout = w_out[0].astype(BF16)
    w1 = w_mlp_in[0].astype(BF16)
    w2 = w_mlp_out[0].astype(BF16)
    cw = jnp.pad(conv_w[0].astype(F32), ((0, 1), (0, 0)))
    gmix, gq, gkv = row2(norm_mix[0]), row2(norm_q[0]), row2(norm_kv[0])
    bg, cb = row2(b_gate[0]), row2(conv_b[0])
    lng, lnb = row2(conv_ln_g[0]), row2(conv_ln_b[0])
    gmlp, gfin = row2(norm_mlp[0]), row2(norm_final)

    max_len = max(x_prompt.shape[1], x_sample.shape[1], META_ROWS)
    tab = _rope_table(N_META + max_len)

    meta = jnp.pad(meta_tokens.astype(F32), ((0, META_ROWS - N_META), (0, 0)))[None]
    _, k_meta, vt_meta, c_meta = _project(meta, tab[:META_ROWS], gmix, wa, gq, wq, gkv, wkv, META_ROWS)

    def encode(x):
        L = x.shape[1]
        q, k, vt, c = _project(x, tab[N_META:N_META + L], gmix, wa, gq, wq, gkv, wkv, 512)
        o = _attention(q, k, vt, k_meta, vt_meta, 2048, 512)
        h = _mix(x, o, c, c_meta, gmix, wg, bg, wo, cw, cb, lng, lnb, wco, wout, 256, 64)
        return _mlp(h, gmlp, w1, w2, gfin, 512)

    return encode(x_prompt), encode(x_sample)
```

```python
import functools
import math

import jax
import jax.numpy as jnp
from jax import lax
from jax.experimental import pallas as pl
from jax.experimental.pallas import tpu as pltpu

D_MODEL = 1024
N_META = 16
N_HEADS = 8
NOPE_DIM = 128
ROPE_DIM = 64
V_DIM = 128
Q_LORA = 512
KV_LORA = 256
ROPE_THETA = 10000.0
D_CONV = D_MODEL
CONV_WIDTH = 31
CONV_PAD = CONV_WIDTH // 2
D_FF = 4 * D_MODEL
N_GATE = 2 * D_MODEL
EPS = 1e-6

LANES = 128
HEAD_W = 2 * LANES
META_ROWS = LANES
HALO = 16
LAT_W = Q_LORA + KV_LORA + LANES
N_CBLK = D_CONV // LANES
Q_SCALE = math.log2(math.e) / math.sqrt(NOPE_DIM + ROPE_DIM)
NEG_BIG = -1e30
VMEM_LIMIT = 56 * 1024 * 1024
ATTN_SUB = 256

F32 = jnp.float32
BF16 = jnp.bfloat16


def _rms_f32(x, g):
    return x * lax.rsqrt(jnp.mean(x * x, axis=-1, keepdims=True) + EPS) * g


def _dot(a, b):
    return jnp.dot(a, b, preferred_element_type=F32)


def _dot_nt(a, b):
    return lax.dot_general(a, b, (((1,), (1,)), ((), ())), preferred_element_type=F32)


def _rope_pair_sum(t):
    return t + pltpu.roll(t, ROPE_DIM, 1)


def _resident(shape):
    return pl.BlockSpec(shape, lambda *_: (0,) * len(shape), pipeline_mode=pl.Buffered(1))


def _proj_kernel(x_ref, tab_ref, gmix_ref, wa_ref, gq_ref, wq_ref, gkv_ref, wkv_ref,
                 q_ref, k_ref, vt_ref, c_ref):
    x = x_ref[0]
    u = _rms_f32(x, gmix_ref[...]).astype(BF16)
    tab = tab_ref[...]
    lat = _dot(u, wa_ref[:, :LAT_W])

    rq = _rms_f32(lat[:, :Q_LORA], gq_ref[...]).astype(BF16)
    qa = _dot(rq, wq_ref[...])
    for h in range(N_HEADS):
        base = h * HEAD_W
        qn = qa[:, base:base + LANES] * Q_SCALE
        qr = _rope_pair_sum(qa[:, base + LANES:base + HEAD_W] * tab) * Q_SCALE
        q_ref[0, h, :, :LANES] = qn.astype(BF16)
        q_ref[0, h, :, LANES:] = qr.astype(BF16)

    kr = _rope_pair_sum(lat[:, Q_LORA + KV_LORA:LAT_W] * tab)
    lane = lax.broadcasted_iota(jnp.int32, kr.shape, 1)
    kr = jnp.where(lane < ROPE_DIM, kr, 0.0).astype(BF16)

    rkv = _rms_f32(lat[:, Q_LORA:Q_LORA + KV_LORA], gkv_ref[...]).astype(BF16)
    kv = _dot(rkv, wkv_ref[...])
    for h in range(N_HEADS):
        base = h * HEAD_W
        k_ref[0, h, :, :LANES] = kv[:, base:base + LANES].astype(BF16)
        k_ref[0, h, :, LANES:] = kr
        vt_ref[0, h] = kv[:, base + LANES:base + HEAD_W].T.astype(BF16)

    ab = _dot(u, wa_ref[:, LAT_W:])
    c_ref[0] = ab[:, :D_CONV] * jax.nn.sigmoid(ab[:, D_CONV:])


def _project(x, tab, gmix, wa, gq, wq, gkv, wkv, tm):
    B, L, _ = x.shape
    return pl.pallas_call(
        _proj_kernel,
        grid=(B, L // tm),
        in_specs=[
            pl.BlockSpec((1, tm, D_MODEL), lambda b, i: (b, i, 0)),
            pl.BlockSpec((tm, LANES), lambda b, i: (i, 0)),
            _resident((1, D_MODEL)),
            _resident(wa.shape),
            _resident((1, Q_LORA)),
            _resident(wq.shape),
            _resident((1, KV_LORA)),
            _resident(wkv.shape),
        ],
        out_specs=[
            pl.BlockSpec((1, N_HEADS, tm, HEAD_W), lambda b, i: (b, 0, i, 0)),
            pl.BlockSpec((1, N_HEADS, tm, HEAD_W), lambda b, i: (b, 0, i, 0)),
            pl.BlockSpec((1, N_HEADS, V_DIM, tm), lambda b, i: (b, 0, 0, i)),
            pl.BlockSpec((1, tm, D_CONV), lambda b, i: (b, i, 0)),
        ],
        out_shape=[
            jax.ShapeDtypeStruct((B, N_HEADS, L, HEAD_W), BF16),
            jax.ShapeDtypeStruct((B, N_HEADS, L, HEAD_W), BF16),
            jax.ShapeDtypeStruct((B, N_HEADS, V_DIM, L), BF16),
            jax.ShapeDtypeStruct((B, L, D_CONV), F32),
        ],
        compiler_params=pltpu.CompilerParams(
            dimension_semantics=("arbitrary", "arbitrary"), vmem_limit_bytes=VMEM_LIMIT),
        name="proj",
    )(x, tab, gmix, wa, gq, wq, gkv, wkv)


def _tree(op, xs):
    xs = list(xs)
    while len(xs) > 1:
        xs = [op(xs[i], xs[i + 1]) if i + 1 < len(xs) else xs[i] for i in range(0, len(xs), 2)]
    return xs[0]


def _row_groups(a):
    return [a[r:r + 8] for r in range(0, a.shape[0], 8)]


def _attn_kernel(q_ref, k_ref, vt_ref, km_ref, vmt_ref, o_ref, s_scr, p_scr, *, seq, tk, n_sub):
    n_chunks = seq // tk
    n_groups = (seq + META_ROWS) // LANES
    groups_per_chunk = -(-n_groups // (n_chunks + 1))
    key_row = lax.broadcasted_iota(jnp.int32, (META_ROWS, ATTN_SUB), 0)

    def q_rows(t):
        return q_ref[0, 0, t * ATTN_SUB:(t + 1) * ATTN_SUB, :]

    def score_meta(t):
        sm = jnp.where(key_row < N_META, _dot_nt(km_ref[0, 0], q_rows(t)), NEG_BIG)
        s_scr[t % 2, seq:, :] = sm
        return _tree(jnp.maximum, _row_groups(sm))

    def score_chunk(t, j, m_acc):
        s = _dot_nt(k_ref[0, 0, j * tk:(j + 1) * tk, :], q_rows(t))
        s_scr[t % 2, j * tk:(j + 1) * tk, :] = s
        return jnp.maximum(m_acc, _tree(jnp.maximum, _row_groups(s)))

    def prob_group(t, g, m, l_acc):
        rows = slice(g * LANES, (g + 1) * LANES)
        p = jnp.exp2(s_scr[t % 2, rows, :] - m)
        p_scr[t % 2, rows, :] = p.astype(BF16)
        return l_acc + _tree(jnp.add, _row_groups(p))

    def weighted_values(t, l_acc):
        l = jnp.sum(l_acc, axis=0, keepdims=True)
        acc = _dot(vt_ref[0, 0], p_scr[t % 2, :seq, :]) + _dot(vmt_ref[0, 0], p_scr[t % 2, seq:, :])
        o_ref[0, t * ATTN_SUB:(t + 1) * ATTN_SUB, :] = (acc / l).T.astype(BF16)

    row_max = {}
    row_sum = {}
    for stage in range(n_sub + 2):
        t_score, t_prob, t_out = stage, stage - 1, stage - 2
        if 0 <= t_out < n_sub:
            weighted_values(t_out, row_sum.pop(t_out))
        do_score = t_score < n_sub
        do_prob = 0 <= t_prob < n_sub
        m_acc = score_meta(t_score) if do_score else None
        l_acc = jnp.zeros((8, ATTN_SUB), F32) if do_prob else None
        g = 0
        for j in range(n_chunks + 1):
            if do_score and j < n_chunks:
                m_acc = score_chunk(t_score, j, m_acc)
            if do_prob:
                for _ in range(groups_per_chunk):
                    if g < n_groups:
                        l_acc = prob_group(t_prob, g, row_max[t_prob], l_acc)
                        g += 1
        if do_score:
            row_max[t_score] = jnp.max(m_acc, axis=0, keepdims=True)
        if do_prob:
            row_sum[t_prob] = l_acc


def _attention(q, k, vt, k_meta, vt_meta, tq, tk):
    B, H, L, _ = q.shape
    kern = functools.partial(_attn_kernel, seq=L, tk=tk, n_sub=tq // ATTN_SUB)
    return pl.pallas_call(
        kern,
        grid=(B, H, L // tq),
        in_specs=[
            pl.BlockSpec((1, 1, tq, HEAD_W), lambda b, h, i: (b, h, i, 0)),
            pl.BlockSpec((1, 1, L, HEAD_W), lambda b, h, i: (b, h, 0, 0)),
            pl.BlockSpec((1, 1, V_DIM, L), lambda b, h, i: (b, h, 0, 0)),
            pl.BlockSpec((1, 1, META_ROWS, HEAD_W), lambda b, h, i: (0, h, 0, 0)),
            pl.BlockSpec((1, 1, V_DIM, META_ROWS), lambda b, h, i: (0, h, 0, 0)),
        ],
        out_specs=pl.BlockSpec((1, tq, V_DIM), lambda b, h, i: (b, i, h)),
        out_shape=jax.ShapeDtypeStruct((B, L, H * V_DIM), BF16),
        scratch_shapes=[
            pltpu.VMEM((2, L + META_ROWS, ATTN_SUB), F32),
            pltpu.VMEM((2, L + META_ROWS, ATTN_SUB), BF16),
        ],
        compiler_params=pltpu.CompilerParams(
            dimension_semantics=("arbitrary", "arbitrary", "arbitrary"), vmem_limit_bytes=VMEM_LIMIT),
        name="attn",
    )(q, k, vt, k_meta, vt_meta)


def _mix_kernel(x_ref, o_ref, c_ref, cprev_ref, cnext_ref, cmeta_ref,
                gmix_ref, wg_ref, bg_ref, wo_ref, cw_ref, cb_ref, lng_ref, lnb_ref, wco_ref, wout_ref,
                h_ref, ext_scr, conv_scr, *, tm, rows):
    i = pl.program_id(1)
    last = pl.num_programs(1) - 1

    prev = jnp.where(i == 0, cmeta_ref[0], cprev_ref[0])
    nxt = jnp.where(i == last, 0.0, cnext_ref[0])
    for cb in range(N_CBLK):
        cols = slice(cb * LANES, (cb + 1) * LANES)
        ext_scr[cb, :HALO, :] = prev[:, cols]
        ext_scr[cb, HALO:HALO + tm, :] = c_ref[0, :, cols]
        ext_scr[cb, HALO + tm:, :] = nxt[:, cols]

    off = HALO - CONV_PAD
    for r in range(tm // rows):
        for cb in range(N_CBLK):
            cols = slice(cb * LANES, (cb + 1) * LANES)
            acc = jnp.broadcast_to(cb_ref[:, cols], (rows, LANES))
            for k in range(CONV_WIDTH):
                start = r * rows + off + k
                acc = acc + ext_scr[cb, start:start + rows, :] * cw_ref[k:k + 1, cols]
            conv_scr[r * rows:(r + 1) * rows, cols] = acc

    cv = conv_scr[...]
    mu = jnp.mean(cv, axis=-1, keepdims=True)
    var = jnp.mean(jnp.square(cv - mu), axis=-1, keepdims=True)
    y = (cv - mu) * lax.rsqrt(var + EPS) * lng_ref[...] + lnb_ref[...]
    conv = _dot((y * jax.nn.sigmoid(y)).astype(BF16), wco_ref[...])

    x = x_ref[0]
    u = _rms_f32(x, gmix_ref[...]).astype(BF16)
    gates = jax.nn.sigmoid(_dot(u, wg_ref[...]) + bg_ref[...])
    attn = _dot(o_ref[0], wo_ref[...])
    merged = gates[:, :D_MODEL] * attn + gates[:, D_MODEL:] * conv
    h_ref[0] = x + _dot(merged.astype(BF16), wout_ref[...])


def _mix(x, o, c, c_meta, gmix, wg, bg, wo, cw, cb, lng, lnb, wco, wout, tm, rows):
    B, L, _ = x.shape
    hb = tm // HALO
    n_halo = L // HALO
    row = lambda b, i: (b, i, 0)
    kern = functools.partial(_mix_kernel, tm=tm, rows=rows)
    return pl.pallas_call(
        kern,
        grid=(B, L // tm),
        in_specs=[
            pl.BlockSpec((1, tm, D_MODEL), row),
            pl.BlockSpec((1, tm, D_MODEL), row),
            pl.BlockSpec((1, tm, D_CONV), row),
            pl.BlockSpec((1, HALO, D_CONV), lambda b, i: (b, jnp.maximum(i * hb - 1, 0), 0)),
            pl.BlockSpec((1, HALO, D_CONV), lambda b, i: (b, jnp.minimum((i + 1) * hb, n_halo - 1), 0)),
            pl.BlockSpec((1, HALO, D_CONV), lambda b, i: (0, 0, 0)),
            _resident((1, D_MODEL)),
            _resident(wg.shape),
            _resident((1, N_GATE)),
            _resident(wo.shape),
            _resident(cw.shape),
            _resident((1, D_CONV)),
            _resident((1, D_CONV)),
            _resident((1, D_CONV)),
            _resident(wco.shape),
            _resident(wout.shape),
        ],
        out_specs=pl.BlockSpec((1, tm, D_MODEL), row),
        out_shape=jax.ShapeDtypeStruct((B, L, D_MODEL), F32),
        scratch_shapes=[
            pltpu.VMEM((N_CBLK, tm + 2 * HALO, LANES), F32),
            pltpu.VMEM((tm, D_CONV), F32),
        ],
        compiler_params=pltpu.CompilerParams(
            dimension_semantics=("arbitrary", "arbitrary"), vmem_limit_bytes=VMEM_LIMIT),
        name="mix",
    )(x, o, c, c, c, c_meta, gmix, wg, bg, wo, cw, cb, lng, lnb, wco, wout)


def _mlp_kernel(h_ref, gmlp_ref, w1_ref, w2_ref, gfin_ref, y_ref):
    h = h_ref[0]
    u = _rms_f32(h, gmlp_ref[...]).astype(BF16)
    hid = jnp.square(jnp.maximum(_dot(u, w1_ref[...]), 0.0)).astype(BF16)
    h2 = h + _dot(hid, w2_ref[...])
    y_ref[0] = _rms_f32(h2, gfin_ref[...])


def _mlp(h, gmlp, w1, w2, gfin, tm):
    B, L, _ = h.shape
    row = lambda b, i: (b, i, 0)
    return pl.pallas_call(
        _mlp_kernel,
        grid=(B, L // tm),
        in_specs=[
            pl.BlockSpec((1, tm, D_MODEL), row),
            _resident((1, D_MODEL)),
            _resident(w1.shape),
            _resident(w2.shape),
            _resident((1, D_MODEL)),
        ],
        out_specs=pl.BlockSpec((1, tm, D_MODEL), row),
        out_shape=jax.ShapeDtypeStruct((B, L, D_MODEL), F32),
        compiler_params=pltpu.CompilerParams(
            dimension_semantics=("arbitrary", "arbitrary"), vmem_limit_bytes=VMEM_LIMIT),
        name="mlp",
    )(h, gmlp, w1, w2, gfin)


def _rotate_half_cols(w):
    half = ROPE_DIM // 2
    return jnp.concatenate([-w[..., half:], w[..., :half]], axis=-1)


def _rope_table(n):
    inv = ROPE_THETA ** (-jnp.arange(0, ROPE_DIM, 2, dtype=F32) / ROPE_DIM)
    ang = jnp.arange(n, dtype=F32)[:, None] * inv[None, :]
    cos, sin = jnp.cos(ang), jnp.sin(ang)
    return jnp.concatenate([cos, cos, sin, sin], axis=-1)


def kernel(x_prompt, x_sample, meta_tokens, norm_mix, w_in, b_gate, norm_q, w_uq, norm_kv, w_ukv,
           w_o_attn, conv_w, conv_b, conv_ln_g, conv_ln_b, w_conv_out, w_out, norm_mlp, w_mlp_in,
           w_mlp_out, norm_final):
    assert w_in.shape[0] == 1, "single-layer block"
    row2 = lambda a: a.reshape(1, -1).astype(F32)

    w_in0 = w_in[0]
    o_kr = Q_LORA + KV_LORA
    o_conv = o_kr + ROPE_DIM
    o_gate = o_conv + 2 * D_CONV
    w_kr = w_in0[:, o_kr:o_conv]
    wa = jnp.concatenate(
        [w_in0[:, :o_kr], w_kr, _rotate_half_cols(w_kr), w_in0[:, o_conv:o_gate]], axis=1).astype(BF16)
    wg = w_in0[:, o_gate:].astype(BF16)
    wq3 = w_uq[0].reshape(Q_LORA, N_HEADS, NOPE_DIM + ROPE_DIM)
    wq_r = wq3[..., NOPE_DIM:]
    wq = jnp.concatenate([wq3[..., :NOPE_DIM], wq_r, _rotate_half_cols(wq_r)], axis=-1)
    wq = wq.reshape(Q_LORA, N_HEADS * HEAD_W).astype(BF16)
    wkv = w_ukv[0].astype(BF16)
    wo = w_o_attn[0].astype(BF16)
    wco = w_conv_out[0].astype(BF16)
    wout = w_out[0].astype(BF16)
    w1 = w_mlp_in[0].astype(BF16)
    w2 = w_mlp_out[0].astype(BF16)
    cw = jnp.pad(conv_w[0].astype(F32), ((0, 1), (0, 0)))
    gmix, gq, gkv = row2(norm_mix[0]), row2(norm_q[0]), row2(norm_kv[0])
    bg, cb = row2(b_gate[0]), row2(conv_b[0])
    lng, lnb = row2(conv_ln_g[0]), row2(conv_ln_b[0])
    gmlp, gfin = row2(norm_mlp[0]), row2(norm_final)

    max_len = max(x_prompt.shape[1], x_sample.shape[1], META_ROWS)
    tab = _rope_table(N_META + max_len)

    meta = jnp.pad(meta_tokens.astype(F32), ((0, META_ROWS - N_META), (0, 0)))[None]
    _, k_meta, vt_meta, c_meta = _project(meta, tab[:META_ROWS], gmix, wa, gq, wq, gkv, wkv, META_ROWS)

    def encode(x):
        L = x.shape[1]
        q, k, vt, c = _project(x, tab[N_META:N_META + L], gmix, wa, gq, wq, gkv, wkv, 512)
        o = _attention(q, k, vt, k_meta, vt_meta, L, 512)
        h = _mix(x, o, c, c_meta, gmix, wg, bg, wo, cw, cb, lng, lnb, wco, wout, 256, 64)
        return _mlp(h, gmlp, w1, w2, gfin, 512)

    return encode(x_prompt), encode(x_sample)
```
